```python
import jax, jax.numpy as jnp
from jax import lax
import numpy as np

D_MODEL = 1024
BATCH = 2
SEQ = 8192
DEPTH = 4

EPS = 1e-5
HEAD_DIM = 64
SSD_HEADS = 8
SSD_WIDTH = SSD_HEADS * HEAD_DIM
SSD_GROUPS = 2
SSD_STATE = 128
SSD_CHUNK = 128
CONV_K = 4
XBC_WIDTH = SSD_WIDTH + 2 * SSD_GROUPS * SSD_STATE
MOBA_HEADS = 4
MOBA_WIDTH = MOBA_HEADS * HEAD_DIM
MOBA_BLOCK = 256
MOBA_TOPK = 3
MOBA_QBLOCK = 64
SWA_HEADS = 4
SWA_KV_HEADS = 2
SWA_WIDTH = SWA_HEADS * HEAD_DIM
SWA_KV_WIDTH = SWA_KV_HEADS * HEAD_DIM
SWA_WINDOW = 128
SWA_BLOCK = 128
MIX_WIDTH = SSD_WIDTH + MOBA_WIDTH + SWA_WIDTH
N_IN = SSD_WIDTH + XBC_WIDTH + SSD_HEADS + 3 * MOBA_WIDTH + SWA_WIDTH + 2 * SWA_KV_WIDTH
PEER_HEADS = 8
PEER_TOPK = 16
PEER_NKEYS = 128
PEER_EXPERTS = PEER_NKEYS * PEER_NKEYS
PEER_DKEY = 256
PEER_HALF = PEER_DKEY // 2
PEER_CHUNK = 128

kernel_name = "hymba_ssd_moba_swa_peer_adaln"


def rms_norm(x, g):
    x32 = x.astype(jnp.float32)
    y = x32 * lax.rsqrt(jnp.mean(x32 * x32, axis=-1, keepdims=True) + EPS)
    return y.astype(x.dtype) * g.astype(x.dtype)


def causal_depthwise_conv(x, w, b):
    y = lax.conv_general_dilated(
        x, w.astype(x.dtype)[:, None, :], window_strides=(1,), padding=[(CONV_K - 1, 0)],
        dimension_numbers=("NWC", "WIO", "NWC"), feature_group_count=x.shape[-1])
    return y + b.astype(x.dtype)


def segsum(a):
    t = a.shape[-1]
    xr = jnp.broadcast_to(a[..., None], a.shape + (t,))
    xr = jnp.where(jnp.tril(jnp.ones((t, t), bool), -1), xr, 0.0)
    cs = jnp.cumsum(xr, axis=-2)
    return jnp.where(jnp.tril(jnp.ones((t, t), bool)), cs, -jnp.inf)


def ssd_mixer(z, xbc, dt_raw, conv_w, conv_b, dt_bias, a_log, d_skip, norm_g):
    f32 = jnp.float32
    b, s, _ = z.shape
    G, R, L, P, N = SSD_GROUPS, SSD_HEADS // SSD_GROUPS, SSD_CHUNK, HEAD_DIM, SSD_STATE
    nc = s // L
    xbc = jax.nn.silu(causal_depthwise_conv(xbc, conv_w, conv_b))
    xs, bm, cm = jnp.split(xbc, [SSD_WIDTH, SSD_WIDTH + G * N], axis=-1)
    dt = jax.nn.softplus(dt_raw.astype(f32) + dt_bias.astype(f32))
    a_head = -jnp.exp(a_log.astype(f32))
    xh = xs.reshape(b, nc, L, G, R, P)
    xdt = xh.astype(f32) * dt.reshape(b, nc, L, G, R)[..., None]
    bc = bm.reshape(b, nc, L, G, N)
    cc = cm.reshape(b, nc, L, G, N)
    a = (dt * a_head).reshape(b, nc, L, G, R).transpose(0, 3, 4, 1, 2)
    a_cs = jnp.cumsum(a, axis=-1)
    decay = jnp.exp(segsum(a))
    cb = jnp.einsum("bclgn,bcsgn->bgcls", cc, bc)
    y_diag = jnp.einsum("bgcls,bgrcls,bcsgrp->bclgrp", cb, decay, xdt)
    decay_states = jnp.exp(a_cs[..., -1:] - a_cs)
    states = jnp.einsum("bclgn,bgrcl,bclgrp->bcgrpn", bc, decay_states, xdt)
    states = jnp.concatenate([jnp.zeros_like(states[:, :1]), states], axis=1)
    last = jnp.pad(a_cs[..., -1], ((0, 0), (0, 0), (0, 0), (1, 0)))
    chunk_decay = jnp.exp(segsum(last))
    states = jnp.einsum("bgrzc,bcgrpn->bzgrpn", chunk_decay, states)[:, :-1]
    y_off = jnp.einsum("bclgn,bcgrpn,bgrcl->bclgrp", cc, states, jnp.exp(a_cs))
    y = (y_diag + y_off).reshape(b, s, SSD_HEADS, P) \
        + d_skip.astype(f32)[:, None] * xs.reshape(b, s, SSD_HEADS, P).astype(f32)
    y = y.reshape(b, s, SSD_WIDTH).astype(z.dtype)
    return rms_norm(y * jax.nn.silu(z), norm_g)


def moba_attention(q, k, v, norm_g):
    f32 = jnp.float32
    b, s, _ = q.shape
    H, Dh, BL, QB = MOBA_HEADS, HEAD_DIM, MOBA_BLOCK, MOBA_QBLOCK
    nb = -(-s // BL)
    pad = nb * BL - s
    q, k, v = (t.reshape(b, s, H, Dh).transpose(0, 2, 1, 3) for t in (q, k, v))
    k = jnp.pad(k, ((0, 0), (0, 0), (0, pad), (0, 0)))
    v = jnp.pad(v, ((0, 0), (0, 0), (0, pad), (0, 0)))
    kb = k.reshape(b, H, nb, BL, Dh)
    vb = v.reshape(b, H, nb, BL, Dh)
    k_mean = jnp.mean(kb.astype(f32), axis=3)
    gate = jnp.einsum("bhsd,bhnd->bhsn", q.astype(f32), k_mean)
    past = jnp.arange(nb)[None, :] < (jnp.arange(s) // BL)[:, None]
    gate = jnp.where(past, gate, -jnp.inf)
    topk = min(MOBA_TOPK, nb)
    _, sel = lax.top_k(gate, topk)
    nq = s // QB
    q_chunks = q.reshape(b, H, nq, QB, Dh).transpose(2, 0, 1, 3, 4)
    sel_chunks = sel.reshape(b, H, nq, QB, topk).transpose(2, 0, 1, 3, 4)
    bi = jnp.arange(b)[:, None, None, None]
    hi = jnp.arange(H)[None, :, None, None]
    scale = Dh ** -0.5

    def one_chunk(args):
        qc, selc, ci = args
        q0 = ci * QB
        blk = q0 // BL
        q_pos = q0 + jnp.arange(QB)
        k_sel = kb[bi, hi, selc]
        v_sel = vb[bi, hi, selc]
        s_sel = jnp.einsum("bhqd,bhqjkd->bhqjk", qc, k_sel).astype(f32) * scale
        valid = jnp.arange(topk) < blk
        s_sel = jnp.where(valid[:, None], s_sel, -jnp.inf).reshape(b, H, QB, topk * BL)
        start = blk * BL
        k_own = lax.dynamic_slice_in_dim(k, start, BL, axis=2)
        v_own = lax.dynamic_slice_in_dim(v, start, BL, axis=2)
        s_own = jnp.einsum("bhqd,bhkd->bhqk", qc, k_own).astype(f32) * scale
        k_pos = start + jnp.arange(BL)
        s_own = jnp.where(k_pos[None, :] <= q_pos[:, None], s_own, -jnp.inf)
        p = jax.nn.softmax(jnp.concatenate([s_sel, s_own], axis=-1), axis=-1).astype(v.dtype)
        p_sel = p[..., :topk * BL].reshape(b, H, QB, topk, BL)
        p_own = p[..., topk * BL:]
        return jnp.einsum("bhqjk,bhqjkd->bhqd", p_sel, v_sel) + jnp.einsum("bhqk,bhkd->bhqd", p_own, v_own)

    out = lax.map(one_chunk, (q_chunks, sel_chunks, jnp.arange(nq)))
    out = out.transpose(1, 0, 3, 2, 4).reshape(b, s, H * Dh)
    return rms_norm(out, norm_g)


def swa_attention(q, k, v, sinks, norm_g):
    f32 = jnp.float32
    b, s, _ = q.shape
    W, G, R, Dh = SWA_BLOCK, SWA_KV_HEADS, SWA_HEADS // SWA_KV_HEADS, HEAD_DIM
    nb = s // W
    qb = q.reshape(b, nb, W, G, R, Dh)
    kb = k.reshape(b, nb, W, G, Dh)
    vb = v.reshape(b, nb, W, G, Dh)
    shift = lambda t: jnp.concatenate([jnp.zeros_like(t[:, :1]), t[:, :-1]], axis=1)
    k_band = jnp.concatenate([shift(kb), kb], axis=2)
    v_band = jnp.concatenate([shift(vb), vb], axis=2)
    scores = jnp.einsum("bnqgrd,bnkgd->bngrqk", qb, k_band).astype(f32) * (Dh ** -0.5)
    i = jnp.arange(W)[:, None]
    j = jnp.arange(2 * W)[None, :]
    rel = i + W - j
    in_win = (rel >= 0) & (rel < SWA_WINDOW)
    has_prev = (jnp.arange(nb) > 0)[:, None, None] | (j >= W)[None]
    mask = in_win[None] & has_prev
    scores = jnp.where(mask[None, :, None, None], scores, -jnp.inf)
    sink = jnp.broadcast_to(sinks.astype(f32).reshape(G, R)[None, None, :, :, None, None],
                            scores.shape[:-1] + (1,))
    p = jax.nn.softmax(jnp.concatenate([scores, sink], axis=-1), axis=-1)[..., :-1].astype(v.dtype)
    out = jnp.einsum("bngrqk,bnkgd->bnqgrd", p, v_band).reshape(b, s, SWA_WIDTH)
    return rms_norm(out, norm_g)


def peer_ffn(h, wq, k1, k2, u, v):
    f32 = jnp.float32
    b, s, d = h.shape
    nch = (b * s) // PEER_CHUNK
    hc = h.reshape(nch, PEER_CHUNK, d)

    def one_chunk(xc):
        q = (xc @ wq).reshape(PEER_CHUNK, PEER_HEADS, 2, PEER_HALF).astype(f32)
        s1 = jnp.einsum("thd,nd->thn", q[:, :, 0], k1.astype(f32))
        s2 = jnp.einsum("thd,nd->thn", q[:, :, 1], k2.astype(f32))
        v1, i1 = lax.top_k(s1, PEER_TOPK)
        v2, i2 = lax.top_k(s2, PEER_TOPK)
        cand = (v1[..., :, None] + v2[..., None, :]).reshape(PEER_CHUNK, PEER_HEADS, PEER_TOPK * PEER_TOPK)
        sc, ci = lax.top_k(cand, PEER_TOPK)
        e = jnp.take_along_axis(i1, ci // PEER_TOPK, axis=-1) * PEER_NKEYS \
            + jnp.take_along_axis(i2, ci % PEER_TOPK, axis=-1)
        g = jax.nn.softmax(sc, axis=-1)
        u_sel = u[e]
        v_sel = v[e]
        pre = jnp.einsum("td,thkd->thk", xc, u_sel).astype(f32)
        act = (jax.nn.gelu(pre, approximate=False) * g).astype(v.dtype)
        return jnp.einsum("thk,thkd->td", act, v_sel)

    return lax.map(one_chunk, hc).reshape(b, s, d)


def setup_inputs(seed: int = 0) -> dict:
    key = jax.random.key(seed)
    ks = jax.random.split(key, 24)
    f32 = jnp.float32
    nrm = lambda k, shape, scale: jax.random.normal(k, shape, f32) * scale
    L, D = DEPTH, D_MODEL
    dt0 = jnp.exp(jax.random.uniform(ks[8], (L, SSD_HEADS), f32) * (np.log(0.1) - np.log(0.001)) + np.log(0.001))
    return {
        "x": nrm(ks[0], (BATCH, SEQ, D), 1.0),
        "c": nrm(ks[1], (BATCH, D), 1.0),
        "ada_w": nrm(ks[2], (L, D, 6 * D), 0.5 * D ** -0.5),
        "ada_b": nrm(ks[3], (L, 6 * D), 0.01),
        "norm1_g": 1.0 + nrm(ks[4], (L, D), 0.02),
        "norm2_g": 1.0 + nrm(ks[5], (L, D), 0.02),
        "w_in": nrm(ks[6], (L, D, N_IN), D ** -0.5),
        "conv_w": nrm(ks[7], (L, CONV_K, XBC_WIDTH), CONV_K ** -0.5),
        "conv_b": nrm(ks[9], (L, XBC_WIDTH), 0.01),
        "dt_bias": dt0 + jnp.log(-jnp.expm1(-dt0)),
        "a_log": jnp.log(jax.random.uniform(ks[10], (L, SSD_HEADS), f32, 1.0, 16.0)),
        "d_skip": 1.0 + nrm(ks[11], (L, SSD_HEADS), 0.1),
        "ssd_norm_g": 1.0 + nrm(ks[12], (L, SSD_WIDTH), 0.02),
        "moba_norm_g": 1.0 + nrm(ks[13], (L, MOBA_WIDTH), 0.02),
        "swa_sinks": nrm(ks[14], (L, SWA_HEADS), 0.5),
        "swa_norm_g": 1.0 + nrm(ks[15], (L, SWA_WIDTH), 0.02),
        "w_out": nrm(ks[16], (L, MIX_WIDTH, D), MIX_WIDTH ** -0.5),
        "peer_wq": nrm(ks[17], (L, D, PEER_HEADS * PEER_DKEY), D ** -0.5),
        "peer_k1": nrm(ks[18], (L, PEER_NKEYS, PEER_HALF), PEER_HALF ** -0.5),
        "peer_k2": nrm(ks[19], (L, PEER_NKEYS, PEER_HALF), PEER_HALF ** -0.5),
        "peer_u": nrm(ks[20], (L, PEER_EXPERTS, D), D ** -0.5),
        "peer_v": nrm(ks[21], (L, PEER_EXPERTS, D), PEER_HEADS ** -0.5),
        "final_g": 1.0 + nrm(ks[22], (D,), 0.02),
    }


def reference(x, c, ada_w, ada_b, norm1_g, norm2_g, w_in, conv_w, conv_b, dt_bias, a_log, d_skip,
              ssd_norm_g, moba_norm_g, swa_sinks, swa_norm_g, w_out, peer_wq, peer_k1, peer_k2,
              peer_u, peer_v, final_g):
    in_widths = [SSD_WIDTH, XBC_WIDTH, SSD_HEADS, MOBA_WIDTH, MOBA_WIDTH, MOBA_WIDTH,
                 SWA_WIDTH, SWA_KV_WIDTH, SWA_KV_WIDTH]
    offsets = np.cumsum(in_widths)[:-1].tolist()
    cond = jax.nn.silu(c)
    for l in range(DEPTH):
        mod = cond @ ada_w[l] + ada_b[l]
        sh1, sc1, g1, sh2, sc2, g2 = [m[:, None, :] for m in jnp.split(mod, 6, axis=-1)]
        h = rms_norm(x, norm1_g[l]) * (1.0 + sc1) + sh1
        proj = h @ w_in[l]
        z, xbc, dt_raw, mq, mk, mv, sq, sk, sv = jnp.split(proj, offsets, axis=-1)
        y_ssd = ssd_mixer(z, xbc, dt_raw, conv_w[l], conv_b[l], dt_bias[l], a_log[l], d_skip[l], ssd_norm_g[l])
        y_moba = moba_attention(mq, mk, mv, moba_norm_g[l])
        y_swa = swa_attention(sq, sk, sv, swa_sinks[l], swa_norm_g[l])
        y = jnp.concatenate([y_ssd, y_moba, y_swa], axis=-1) @ w_out[l]
        x = x + g1 * y
        h = rms_norm(x, norm2_g[l]) * (1.0 + sc2) + sh2
        x = x + g2 * peer_ffn(h, peer_wq[l], peer_k1[l], peer_k2[l], peer_u[l], peer_v[l])
    return rms_norm(x, final_g)
```

```python
import functools

import numpy as np
import jax
import jax.numpy as jnp
from jax import lax
from jax.experimental import pallas as pl
from jax.experimental.pallas import tpu as pltpu

F32 = jnp.float32
BF16 = jnp.bfloat16
HIGHEST = lax.Precision.HIGHEST

EPS = 1e-5
D_MODEL = 1024
HEAD_DIM = 64
SSD_HEADS = 8
SSD_WIDTH = 512
SSD_GROUPS = 2
SSD_STATE = 128
SSD_CHUNK = 128
CONV_K = 4
XBC_WIDTH = 1024
MOBA_HEADS = 4
MOBA_WIDTH = 256
MOBA_BLOCK = 256
MOBA_TOPK = 3
SWA_HEADS = 4
SWA_KV_HEADS = 2
SWA_WIDTH = 256
SWA_KV_WIDTH = 128
SWA_BLOCK = 128
PEER_HEADS = 8
PEER_TOPK = 16
PEER_NKEYS = 128
PEER_EXPERTS = PEER_NKEYS * PEER_NKEYS
PEER_HALF = 128

DT_PAD = 128
N_IN_PAD = SSD_WIDTH + XBC_WIDTH + 3 * MOBA_WIDTH + SWA_WIDTH + 2 * SWA_KV_WIDTH + DT_PAD
MASKED = -1e30

VMEM_LIMIT = 56 * 1024 * 1024


def _cparams(sem):
    return pltpu.CompilerParams(dimension_semantics=sem, vmem_limit_bytes=VMEM_LIMIT)


def _dot(a, b, precision=None):
    return jnp.dot(a, b, preferred_element_type=F32, precision=precision)


def _dot_nt(a, b, precision=None):
    return lax.dot_general(a, b, (((1,), (1,)), ((), ())), preferred_element_type=F32,
                           precision=precision)


def _dot_tn(a, b, precision=None):
    return lax.dot_general(a, b, (((0,), (0,)), ((), ())), preferred_element_type=F32,
                           precision=precision)


def _silu(x):
    return x / (1.0 + jnp.exp(-x))


def _rms_rows(x, g):
    ms = jnp.mean(x * x, axis=-1, keepdims=True)
    return x * lax.rsqrt(ms + EPS) * g


def _ada_kernel(c_ref, w_ref, b_ref, o_ref):
    cond = _silu(c_ref[...])
    o_ref[...] = _dot(cond, w_ref[...], HIGHEST) + b_ref[...]


def ada_mod(c, ada_w, ada_b, tn=2048):
    nl, d, n6 = ada_w.shape
    b = c.shape[0]
    return pl.pallas_call(
        _ada_kernel,
        grid=(nl, n6 // tn),
        in_specs=[
            pl.BlockSpec((b, d), lambda l, j: (0, 0)),
            pl.BlockSpec((None, d, tn), lambda l, j: (l, 0, j)),
            pl.BlockSpec((None, 1, tn), lambda l, j: (l, 0, j)),
        ],
        out_specs=pl.BlockSpec((None, b, tn), lambda l, j: (l, 0, j)),
        out_shape=jax.ShapeDtypeStruct((nl, b, n6), F32),
        compiler_params=_cparams(("arbitrary", "arbitrary")),
        name="ada_mod",
    )(c, ada_w, ada_b.reshape(nl, 1, n6))


_O_Z = 0
_O_XBC = _O_Z + SSD_WIDTH
_O_MQKV = _O_XBC + XBC_WIDTH
_O_SQKV = _O_MQKV + 3 * MOBA_WIDTH
_O_DT = _O_SQKV + SWA_WIDTH + 2 * SWA_KV_WIDTH


def _inproj_kernel(x_ref, mod_ref, g_ref, w_ref, z_ref, xbc_ref, dt_ref, mqkv_ref, sqkv_ref):
    h = _rms_rows(x_ref[...], g_ref[...]) * (1.0 + mod_ref[1:2, :]) + mod_ref[0:1, :]
    hb = h.astype(BF16)
    z_ref[...] = _dot(hb, w_ref[:, _O_Z:_O_XBC])
    xbc_ref[...] = _dot(hb, w_ref[:, _O_XBC:_O_MQKV])
    mqkv_ref[...] = _dot(hb, w_ref[:, _O_MQKV:_O_SQKV]).astype(BF16)
    sqkv_ref[...] = _dot(hb, w_ref[:, _O_SQKV:_O_DT]).astype(BF16)
    dt_ref[...] = _dot(hb, w_ref[:, _O_DT:N_IN_PAD])


def in_proj(x, mod, g, w, tm=512):
    b, s, d = x.shape
    tok = lambda n: pl.BlockSpec((None, tm, n), lambda bi, i: (bi, i, 0))
    return pl.pallas_call(
        _inproj_kernel,
        grid=(b, s // tm),
        in_specs=[
            tok(d),
            pl.BlockSpec((None, 6, d), lambda bi, i: (bi, 0, 0)),
            pl.BlockSpec((1, d), lambda bi, i: (0, 0)),
            pl.BlockSpec((d, N_IN_PAD), lambda bi, i: (0, 0)),
        ],
        out_specs=[tok(SSD_WIDTH), tok(XBC_WIDTH), tok(DT_PAD), tok(3 * MOBA_WIDTH),
                   tok(SWA_WIDTH + 2 * SWA_KV_WIDTH)],
        out_shape=[
            jax.ShapeDtypeStruct((b, s, SSD_WIDTH), F32),
            jax.ShapeDtypeStruct((b, s, XBC_WIDTH), F32),
            jax.ShapeDtypeStruct((b, s, DT_PAD), F32),
            jax.ShapeDtypeStruct((b, s, 3 * MOBA_WIDTH), BF16),
            jax.ShapeDtypeStruct((b, s, SWA_WIDTH + 2 * SWA_KV_WIDTH), BF16),
        ],
        compiler_params=_cparams(("arbitrary", "arbitrary")),
        name="in_proj",
    )(x, mod, g, w)


def _ssd_kernel(z_ref, xbc_ref, dt_ref, cw_ref, cb_ref, dtb_ref, alog_ref, dskip_ref, ng_ref,
                o_ref, xpad_ref, state_ref):
    L, N, P = SSD_CHUNK, SSD_STATE, HEAD_DIM
    GW = SSD_WIDTH // SSD_GROUPS
    c = pl.program_id(1)

    @pl.when(c == 0)
    def _():
        xpad_ref[0:8, :] = jnp.zeros((8, XBC_WIDTH), F32)
        state_ref[...] = jnp.zeros_like(state_ref)

    xpad_ref[8:8 + L, :] = xbc_ref[...]
    y = cb_ref[...] + cw_ref[0:1, :] * xpad_ref[5:5 + L, :]
    for k in range(1, CONV_K):
        y = y + cw_ref[k:k + 1, :] * xpad_ref[5 + k:5 + k + L, :]
    xpad_ref[0:8, :] = xpad_ref[L:L + 8, :]
    xc = _silu(y)
    xs = xc[:, :SSD_WIDTH]

    dtr = dt_ref[...] + dtb_ref[...]
    dtv = jnp.maximum(dtr, 0.0) + jnp.log1p(jnp.exp(-jnp.abs(dtr)))
    a = dtv * (-jnp.exp(alog_ref[...]))
    row = lax.broadcasted_iota(jnp.int32, (L, L), 0)
    col = lax.broadcasted_iota(jnp.int32, (L, L), 1)
    causal = col <= row
    a_cs = _dot(causal.astype(F32), a, HIGHEST)
    a_cs_t = a_cs.T
    er = lax.broadcasted_iota(jnp.int32, (DT_PAD, SSD_WIDTH), 0)
    ec = lax.broadcasted_iota(jnp.int32, (DT_PAD, SSD_WIDTH), 1)
    expand = (ec // P == er).astype(F32)
    dt_x = _dot(dtv, expand, HIGHEST)
    acs_x = _dot(a_cs, expand, HIGHEST)
    atot_x = acs_x[L - 1:L, :]

    xdt = xs * dt_x
    w_state = (xdt * jnp.exp(atot_x - acs_x)).astype(BF16)
    decay_in = jnp.exp(acs_x)
    decay_tot = jnp.exp(atot_x)
    lane_head = lax.broadcasted_iota(jnp.int32, (L, GW), 1) // P

    ys = []
    for g in range(SSD_GROUPS):
        bm = xc[:, SSD_WIDTH + g * N:SSD_WIDTH + (g + 1) * N].astype(BF16)
        cm = xc[:, SSD_WIDTH + SSD_GROUPS * N + g * N:SSD_WIDTH + SSD_GROUPS * N + (g + 1) * N].astype(BF16)
        cbm = _dot_nt(cm, bm)
        xdt_g = xdt[:, g * GW:(g + 1) * GW]
        y_g = jnp.zeros((L, GW), F32)
        for r in range(SSD_HEADS // SSD_GROUPS):
            hh = g * (SSD_HEADS // SSD_GROUPS) + r
            seg = jnp.broadcast_to(a_cs[:, hh:hh + 1], (L, L)) - jnp.broadcast_to(a_cs_t[hh:hh + 1, :], (L, L))
            decay = jnp.where(causal, jnp.exp(jnp.where(causal, seg, 0.0)), 0.0)
            m_h = (cbm * decay).astype(BF16)
            x_h = jnp.where(lane_head == r, xdt_g, 0.0).astype(BF16)
            y_g = y_g + _dot(m_h, x_h)
        st = state_ref[g]
        y_g = y_g + decay_in[:, g * GW:(g + 1) * GW] * _dot(cm, st.astype(BF16))
        state_ref[g] = decay_tot[:, g * GW:(g + 1) * GW] * st + _dot_tn(bm, w_state[:, g * GW:(g + 1) * GW])
        ys.append(y_g)
    yfull = jnp.concatenate(ys, axis=-1) + dskip_ref[...] * xs
    gated = yfull * _silu(z_ref[...])
    o_ref[...] = _rms_rows(gated, ng_ref[...]).astype(BF16)


def ssd_mixer(z, xbc, dt, conv_w, conv_b, dt_bias, a_log, d_skip_x, norm_g):
    b, s, _ = z.shape
    L = SSD_CHUNK
    tok = lambda n: pl.BlockSpec((None, L, n), lambda bi, i: (bi, i, 0))
    par = lambda r, n: pl.BlockSpec((r, n), lambda bi, i: (0, 0))
    return pl.pallas_call(
        _ssd_kernel,
        grid=(b, s // L),
        in_specs=[tok(SSD_WIDTH), tok(XBC_WIDTH), tok(DT_PAD), par(CONV_K, XBC_WIDTH), par(1, XBC_WIDTH),
                  par(1, DT_PAD), par(1, DT_PAD), par(1, SSD_WIDTH), par(1, SSD_WIDTH)],
        out_specs=tok(SSD_WIDTH),
        out_shape=jax.ShapeDtypeStruct((b, s, SSD_WIDTH), BF16),
        scratch_shapes=[pltpu.VMEM((L + 8, XBC_WIDTH), F32),
                        pltpu.VMEM((SSD_GROUPS, SSD_STATE, SSD_WIDTH // SSD_GROUPS), F32)],
        compiler_params=_cparams(("arbitrary", "arbitrary")),
        name="ssd_mixer",
    )(z, xbc, dt, conv_w, conv_b, dt_bias, a_log, d_skip_x, norm_g)


def _moba_kernel(q_ref, k_ref, v_ref, ng_ref, o_ref, kmean_ref, selb_ref, out_t_ref):
    BL, Dh = MOBA_BLOCK, HEAD_DIM
    s = k_ref.shape[0]
    nb = s // BL
    i = pl.program_id(1)

    @pl.when(i == 0)
    def _():
        blk = lax.broadcasted_iota(jnp.int32, (nb, s), 0)
        pos = lax.broadcasted_iota(jnp.int32, (nb, s), 1)
        avg = jnp.where(pos // BL == blk, 1.0 / BL, 0.0).astype(BF16)
        kmean_ref[...] = _dot(avg, k_ref[...])

    key_i = lax.broadcasted_iota(jnp.int32, (BL, BL), 0)
    qry_i = lax.broadcasted_iota(jnp.int32, (BL, BL), 1)
    own_bias = jnp.where(key_i <= qry_i, 0.0, MASKED)
    blk_i = lax.broadcasted_iota(jnp.int32, (nb, BL), 0)
    past = blk_i < i

    for h in range(MOBA_HEADS):
        hs = slice(h * Dh, (h + 1) * Dh)
        q_h = q_ref[:, hs] * (Dh ** -0.5)
        gate = _dot_nt(kmean_ref[:, hs], q_h.astype(F32), HIGHEST)
        cur = jnp.where(past, gate, -jnp.inf)
        for _ in range(MOBA_TOPK - 1):
            cur = jnp.where(cur == jnp.max(cur, axis=0, keepdims=True), -jnp.inf, cur)
        thr = jnp.max(cur, axis=0, keepdims=True)
        selb_ref[...] = jnp.where(past & (gate >= thr), 0.0, MASKED)

        def attend(k_blk, v_blk, bias, carry):
            m, l, acc = carry
            sc = _dot_nt(k_blk, q_h) + bias
            m_new = jnp.maximum(m, jnp.max(sc, axis=0, keepdims=True))
            alpha = jnp.exp(m - m_new)
            p = jnp.exp(sc - m_new)
            l = alpha * l + jnp.sum(p, axis=0, keepdims=True)
            acc = alpha * acc + _dot_tn(v_blk, p.astype(BF16))
            return m_new, l, acc

        own = pl.ds(pl.multiple_of(i * BL, BL), BL)
        carry = (jnp.full((1, BL), MASKED, F32), jnp.zeros((1, BL), F32), jnp.zeros((Dh, BL), F32))
        carry = attend(k_ref[own, hs], v_ref[own, hs], own_bias, carry)

        def body(n, carry):
            rows = pl.ds(pl.multiple_of(n * BL, BL), BL)
            return attend(k_ref[rows, hs], v_ref[rows, hs], selb_ref[pl.ds(n, 1), :], carry)

        m, l, acc = lax.fori_loop(0, i, body, carry)
        out_t_ref[hs, :] = acc / l

    o_ref[...] = _rms_rows(out_t_ref[...].T, ng_ref[...]).astype(BF16)


def moba_attention(mqkv, norm_g):
    b, s, _ = mqkv.shape
    BL = MOBA_BLOCK
    nb = s // BL
    return pl.pallas_call(
        _moba_kernel,
        grid=(b, nb),
        in_specs=[
            pl.BlockSpec((None, BL, MOBA_WIDTH), lambda bi, i: (bi, i, 0)),
            pl.BlockSpec((None, s, MOBA_WIDTH), lambda bi, i: (bi, 0, 1)),
            pl.BlockSpec((None, s, MOBA_WIDTH), lambda bi, i: (bi, 0, 2)),
            pl.BlockSpec((1, MOBA_WIDTH), lambda bi, i: (0, 0)),
        ],
        out_specs=pl.BlockSpec((None, BL, MOBA_WIDTH), lambda bi, i: (bi, i, 0)),
        out_shape=jax.ShapeDtypeStruct((b, s, MOBA_WIDTH), BF16),
        scratch_shapes=[pltpu.VMEM((nb, MOBA_WIDTH), F32), pltpu.VMEM((nb, BL), F32),
                        pltpu.VMEM((MOBA_WIDTH, BL), F32)],
        compiler_params=_cparams(("arbitrary", "arbitrary")),
        name="moba_attention",
    )(mqkv, mqkv, mqkv, norm_g)


def _swa_kernel(sink_ref, q_ref, kp_ref, kc_ref, vp_ref, vc_ref, ng_ref, o_ref, out_t_ref):
    W, Dh = SWA_BLOCK, HEAD_DIM
    i = pl.program_id(1)
    key_i = lax.broadcasted_iota(jnp.int32, (2 * W, W), 0)
    qry_i = lax.broadcasted_iota(jnp.int32, (2 * W, W), 1)
    rel = qry_i + W - key_i
    ok = (rel >= 0) & (rel < W) & ((key_i >= W) | (i > 0))
    bias = jnp.where(ok, 0.0, MASKED)
    rep = SWA_HEADS // SWA_KV_HEADS
    for g in range(SWA_KV_HEADS):
        gs = slice(g * Dh, (g + 1) * Dh)
        k_band = jnp.concatenate([kp_ref[:, gs], kc_ref[:, gs]], axis=0)
        v_band = jnp.concatenate([vp_ref[:, gs], vc_ref[:, gs]], axis=0)
        for r in range(rep):
            hq = g * rep + r
            hs = slice(hq * Dh, (hq + 1) * Dh)
            q_h = q_ref[:, hs] * (Dh ** -0.5)
            sc = _dot_nt(k_band, q_h) + bias
            sink = sink_ref[hq]
            m = jnp.maximum(jnp.max(sc, axis=0, keepdims=True), sink)
            p = jnp.exp(sc - m)
            den = jnp.sum(p, axis=0, keepdims=True) + jnp.exp(sink - m)
            out_t_ref[hs, :] = _dot_tn(v_band, p.astype(BF16)) / den
    o_ref[...] = _rms_rows(out_t_ref[...].T, ng_ref[...]).astype(BF16)


def swa_attention(sqkv, sinks, norm_g):
    b, s, _ = sqkv.shape
    W = SWA_BLOCK
    prev = lambda bi, i: jnp.maximum(i - 1, 0)
    return pl.pallas_call(
        _swa_kernel,
        grid=(b, s // W),
        in_specs=[
            pl.BlockSpec(memory_space=pltpu.SMEM),
            pl.BlockSpec((None, W, SWA_WIDTH), lambda bi, i: (bi, i, 0)),
            pl.BlockSpec((None, W, SWA_KV_WIDTH), lambda bi, i: (bi, prev(bi, i), 2)),
            pl.BlockSpec((None, W, SWA_KV_WIDTH), lambda bi, i: (bi, i, 2)),
            pl.BlockSpec((None, W, SWA_KV_WIDTH), lambda bi, i: (bi, prev(bi, i), 3)),
            pl.BlockSpec((None, W, SWA_KV_WIDTH), lambda bi, i: (bi, i, 3)),
            pl.BlockSpec((1, SWA_WIDTH), lambda bi, i: (0, 0)),
        ],
        out_specs=pl.BlockSpec((None, W, SWA_WIDTH), lambda bi, i: (bi, i, 0)),
        out_shape=jax.ShapeDtypeStruct((b, s, SWA_WIDTH), BF16),
        scratch_shapes=[pltpu.VMEM((SWA_WIDTH, W), F32)],
        compiler_params=_cparams(("arbitrary", "arbitrary")),
        name="swa_attention",
    )(sinks, sqkv, sqkv, sqkv, sqkv, sqkv, norm_g)


def _outproj_kernel(x_ref, ys_ref, ym_ref, yw_ref, w_ref, mod_ref, g_ref, xo_ref, h_ref):
    ycat = jnp.concatenate([ys_ref[...], ym_ref[...], yw_ref[...]], axis=-1)
    xn = x_ref[...] + mod_ref[2:3, :] * _dot(ycat, w_ref[...])
    xo_ref[...] = xn
    h = _rms_rows(xn, g_ref[...]) * (1.0 + mod_ref[4:5, :]) + mod_ref[3:4, :]
    h_ref[...] = h.astype(BF16)


def out_proj(x, y_ssd, y_moba, y_swa, w_out, mod, g, tm=512):
    b, s, d = x.shape
    tok = lambda n: pl.BlockSpec((None, tm, n), lambda bi, i: (bi, i, 0))
    return pl.pallas_call(
        _outproj_kernel,
        grid=(b, s // tm),
        in_specs=[tok(d), tok(SSD_WIDTH), tok(MOBA_WIDTH), tok(SWA_WIDTH),
                  pl.BlockSpec((d, d), lambda bi, i: (0, 0)),
                  pl.BlockSpec((None, 6, d), lambda bi, i: (bi, 0, 0)),
                  pl.BlockSpec((1, d), lambda bi, i: (0, 0))],
        out_specs=[tok(d), tok(d)],
        out_shape=[jax.ShapeDtypeStruct((b, s, d), F32), jax.ShapeDtypeStruct((b, s, d), BF16)],
        compiler_params=_cparams(("arbitrary", "arbitrary")),
        name="out_proj",
    )(x, y_ssd, y_moba, y_swa, w_out, mod, g)


_CAND = [(r, c) for r in range(PEER_TOPK) for c in range(PEER_TOPK) if (r + 1) * (c + 1) <= PEER_TOPK]
_CAND_ROWS = -(-len(_CAND) // 8) * 8


def _top_rows(s_t, k):
    rows = []
    cur = s_t
    for _ in range(k):
        m = jnp.max(cur, axis=0, keepdims=True)
        rows.append(m)
        cur = jnp.where(cur == m, -jnp.inf, cur)
    return rows


def _route_kernel(h_ref, wq_ref, k1_ref, k2_ref, s1_ref, c1_ref, s2_ref, e2_ref, th_ref, cand_ref):
    tm = h_ref.shape[0]
    q = _dot(h_ref[...], wq_ref[...])
    k1 = k1_ref[...]
    k2 = k2_ref[...]
    cand_ref[len(_CAND):, :] = jnp.full((_CAND_ROWS - len(_CAND), tm), -jnp.inf, F32)
    for h in range(PEER_HEADS):
        q1 = q[:, (2 * h) * PEER_HALF:(2 * h + 1) * PEER_HALF].astype(BF16)
        q2 = q[:, (2 * h + 1) * PEER_HALF:(2 * h + 2) * PEER_HALF].astype(BF16)
        s1 = _dot_nt(k1, q1)
        s2 = _dot_nt(k2, q2)
        v1 = _top_rows(s1, PEER_TOPK)
        v2 = _top_rows(s2, PEER_TOPK)
        for idx, (r, c) in enumerate(_CAND):
            cand_ref[idx:idx + 1, :] = v1[r] + v2[c]
        cand = cand_ref[...]
        cur = cand
        for _ in range(PEER_TOPK - 1):
            cur = jnp.where(cur == jnp.max(cur, axis=0, keepdims=True), -jnp.inf, cur)
        theta = jnp.max(cur, axis=0, keepdims=True)
        top = v1[0] + v2[0]
        zsum = jnp.sum(jnp.where(cand >= theta, jnp.exp(cand - top), 0.0), axis=0, keepdims=True)
        s1_ref[h] = s1
        s2_ref[h] = s2
        c1_ref[h] = jnp.exp(s1 - v1[0]) * (1.0 / zsum)
        e2_ref[h] = jnp.exp(s2 - v2[0])
        th_ref[h:h + 1, :] = theta


def peer_route(h2, wq, k1, k2, tm=256):
    t, d = h2.shape
    big = lambda: pl.BlockSpec((PEER_HEADS, PEER_NKEYS, tm), lambda i: (0, 0, i))
    bigs = jax.ShapeDtypeStruct((PEER_HEADS, PEER_NKEYS, t), F32)
    return pl.pallas_call(
        _route_kernel,
        grid=(t // tm,),
        in_specs=[
            pl.BlockSpec((tm, d), lambda i: (i, 0)),
            pl.BlockSpec((d, 2 * PEER_HEADS * PEER_HALF), lambda i: (0, 0)),
            pl.BlockSpec((PEER_NKEYS, PEER_HALF), lambda i: (0, 0)),
            pl.BlockSpec((PEER_NKEYS, PEER_HALF), lambda i: (0, 0)),
        ],
        out_specs=[big(), big(), big(), big(), pl.BlockSpec((PEER_HEADS, tm), lambda i: (0, i))],
        out_shape=[bigs, bigs, bigs, bigs, jax.ShapeDtypeStruct((PEER_HEADS, t), F32)],
        scratch_shapes=[pltpu.VMEM((_CAND_ROWS, tm), F32)],
        compiler_params=_cparams(("arbitrary",)),
        name="peer_route",
    )(h2, wq, k1, k2)


_SLAB = 8


def _peer_kernel(h_ref, u_ref, v_ref, s1_ref, c1_ref, s2_ref, e2_ref, th_ref, x_ref, mod_ref,
                 o_ref, pre_ref, act_ref, acc_ref, *, te):
    j = pl.program_id(1)
    tm = h_ref.shape[0]

    @pl.when(j == 0)
    def _():
        acc_ref[...] = jnp.zeros_like(acc_ref)

    pre_ref[...] = _dot_nt(u_ref[...], h_ref[...])

    def group(gi, _):
        i1 = j * (te // PEER_NKEYS) + gi
        s1_rows = [s1_ref[h, pl.ds(i1, 1), :] for h in range(PEER_HEADS)]
        c1_rows = [c1_ref[h, pl.ds(i1, 1), :] for h in range(PEER_HEADS)]
        th_rows = [th_ref[h:h + 1, :] for h in range(PEER_HEADS)]
        for sl in range(PEER_NKEYS // _SLAB):
            rows = pl.ds(sl * _SLAB, _SLAB)
            gsum = jnp.zeros((_SLAB, tm), F32)
            for h in range(PEER_HEADS):
                hit = (s1_rows[h] + s2_ref[h, rows, :]) >= th_rows[h]
                gsum = gsum + jnp.where(hit, e2_ref[h, rows, :], 0.0) * c1_rows[h]
            prow = pl.ds(pl.multiple_of(gi * PEER_NKEYS + sl * _SLAB, _SLAB), _SLAB)
            pre = pre_ref[prow, :]
            act = 0.5 * pre * (1.0 + lax.erf(pre * (2.0 ** -0.5))) * gsum
            act_ref[prow, :] = act
        return 0

    lax.fori_loop(0, te // PEER_NKEYS, group, 0)
    acc_ref[...] += _dot_tn(act_ref[...].astype(BF16), v_ref[...])

    @pl.when(j == pl.num_programs(1) - 1)
    def _():
        o_ref[...] = x_ref[...] + mod_ref[5:6, :] * acc_ref[...]


def peer_experts(h2, u, v, s1, c1, s2, e2, theta, x, mod, seq, tm=512, te=512):
    t, d = h2.shape
    ne = u.shape[0]
    big = lambda: pl.BlockSpec((PEER_HEADS, PEER_NKEYS, tm), lambda i, j: (0, 0, i))
    return pl.pallas_call(
        functools.partial(_peer_kernel, te=te),
        grid=(t // tm, ne // te),
        in_specs=[
            pl.BlockSpec((tm, d), lambda i, j: (i, 0)),
            pl.BlockSpec((te, d), lambda i, j: (j, 0)),
            pl.BlockSpec((te, d), lambda i, j: (j, 0)),
            big(), big(), big(), big(),
            pl.BlockSpec((PEER_HEADS, tm), lambda i, j: (0, i)),
            pl.BlockSpec((tm, d), lambda i, j: (i, 0)),
            pl.BlockSpec((None, 6, d), lambda i, j: ((i * tm) // seq, 0, 0)),
        ],
        out_specs=pl.BlockSpec((tm, d), lambda i, j: (i, 0)),
        out_shape=jax.ShapeDtypeStruct((t, d), F32),
        scratch_shapes=[pltpu.VMEM((te, tm), F32), pltpu.VMEM((te, tm), F32), pltpu.VMEM((tm, d), F32)],
        compiler_params=_cparams(("arbitrary", "arbitrary")),
        name="peer_experts",
    )(h2, u, v, s1, c1, s2, e2, theta, x, mod)


def _final_kernel(x_ref, g_ref, o_ref):
    o_ref[...] = _rms_rows(x_ref[...], g_ref[...])


def final_norm(x, g, tm=1024):
    t, d = x.shape
    return pl.pallas_call(
        _final_kernel,
        grid=(t // tm,),
        in_specs=[pl.BlockSpec((tm, d), lambda i: (i, 0)), pl.BlockSpec((1, d), lambda i: (0, 0))],
        out_specs=pl.BlockSpec((tm, d), lambda i: (i, 0)),
        out_shape=jax.ShapeDtypeStruct((t, d), F32),
        compiler_params=_cparams(("arbitrary",)),
        name="final_norm",
    )(x, g)


def _reorder_w_in(w_in):
    o_dt = SSD_WIDTH + XBC_WIDTH
    o_rest = o_dt + SSD_HEADS
    pad = jnp.zeros(w_in.shape[:-1] + (DT_PAD - SSD_HEADS,), w_in.dtype)
    return jnp.concatenate([w_in[..., :o_dt], w_in[..., o_rest:], w_in[..., o_dt:o_rest], pad], axis=-1)


def _pad_heads(p):
    return jnp.pad(p, ((0, 0), (0, DT_PAD - SSD_HEADS)))


def kernel(x, c, ada_w, ada_b, norm1_g, norm2_g, w_in, conv_w, conv_b, dt_bias, a_log, d_skip, ssd_norm_g, moba_norm_g, swa_sinks, swa_norm_g, w_out, peer_wq, peer_k1, peer_k2, peer_u, peer_v, final_g):
    b, s, d = x.shape
    depth = ada_w.shape[0]
    mod_all = ada_mod(c, ada_w, ada_b).reshape(depth, b, 6, d)
    w_in_b = _reorder_w_in(w_in).astype(BF16)
    w_out_b = w_out.astype(BF16)
    wq_b = peer_wq.astype(BF16)
    k1_b = peer_k1.astype(BF16)
    k2_b = peer_k2.astype(BF16)
    dtb = _pad_heads(dt_bias)
    alog = _pad_heads(a_log)
    dskip_x = jnp.repeat(d_skip, HEAD_DIM, axis=-1)
    for l in range(depth):
        mod = mod_all[l]
        z, xbc, dt, mqkv, sqkv = in_proj(x, mod, norm1_g[l][None], w_in_b[l])
        y_ssd = ssd_mixer(z, xbc, dt, conv_w[l], conv_b[l][None], dtb[l][None], alog[l][None],
                          dskip_x[l][None], ssd_norm_g[l][None])
        y_moba = moba_attention(mqkv, moba_norm_g[l][None])
        y_swa = swa_attention(sqkv, swa_sinks[l], swa_norm_g[l][None])
        x, h2 = out_proj(x, y_ssd, y_moba, y_swa, w_out_b[l], mod, norm2_g[l][None])
        h2f = h2.reshape(b * s, d)
        s1, c1, s2, e2, theta = peer_route(h2f, wq_b[l], k1_b[l], k2_b[l])
        x = peer_experts(h2f, peer_u[l].astype(BF16), peer_v[l].astype(BF16), s1, c1, s2, e2, theta,
                         x.reshape(b * s, d), mod, s).reshape(b, s, d)
    return final_norm(x.reshape(b * s, d), final_g[None]).reshape(b, s, d)
```

```python
import functools

import numpy as np
import jax
import jax.numpy as jnp
from jax import lax
from jax.experimental import pallas as pl
from jax.experimental.pallas import tpu as pltpu

F32 = jnp.float32
BF16 = jnp.bfloat16
HIGHEST = lax.Precision.HIGHEST

EPS = 1e-5
D_MODEL = 1024
HEAD_DIM = 64
SSD_HEADS = 8
SSD_WIDTH = 512
SSD_GROUPS = 2
SSD_STATE = 128
SSD_CHUNK = 128
CONV_K = 4
XBC_WIDTH = 1024
MOBA_HEADS = 4
MOBA_WIDTH = 256
MOBA_BLOCK = 256
MOBA_TOPK = 3
SWA_HEADS = 4
SWA_KV_HEADS = 2
SWA_WIDTH = 256
SWA_KV_WIDTH = 128
SWA_BLOCK = 128
PEER_HEADS = 8
PEER_TOPK = 16
PEER_NKEYS = 128
PEER_EXPERTS = PEER_NKEYS * PEER_NKEYS
PEER_HALF = 128

DT_PAD = 128
N_IN_PAD = SSD_WIDTH + XBC_WIDTH + 3 * MOBA_WIDTH + SWA_WIDTH + 2 * SWA_KV_WIDTH + DT_PAD
MASKED = -1e30

VMEM_LIMIT = 56 * 1024 * 1024


def _cparams(sem):
    return pltpu.CompilerParams(dimension_semantics=sem, vmem_limit_bytes=VMEM_LIMIT)


def _dot(a, b, precision=None):
    return jnp.dot(a, b, preferred_element_type=F32, precision=precision)


def _dot_nt(a, b, precision=None):
    return lax.dot_general(a, b, (((1,), (1,)), ((), ())), preferred_element_type=F32,
                           precision=precision)


def _dot_tn(a, b, precision=None):
    return lax.dot_general(a, b, (((0,), (0,)), ((), ())), preferred_element_type=F32,
                           precision=precision)


def _silu(x):
    return x / (1.0 + jnp.exp(-x))


def _rms_rows(x, g):
    ms = jnp.mean(x * x, axis=-1, keepdims=True)
    return x * lax.rsqrt(ms + EPS) * g


def _ada_kernel(c_ref, w_ref, b_ref, o_ref):
    cond = _silu(c_ref[...])
    o_ref[...] = _dot(cond, w_ref[...], HIGHEST) + b_ref[...]


def ada_mod(c, ada_w, ada_b, tn=2048):
    nl, d, n6 = ada_w.shape
    b = c.shape[0]
    return pl.pallas_call(
        _ada_kernel,
        grid=(nl, n6 // tn),
        in_specs=[
            pl.BlockSpec((b, d), lambda l, j: (0, 0)),
            pl.BlockSpec((None, d, tn), lambda l, j: (l, 0, j)),
            pl.BlockSpec((None, 1, tn), lambda l, j: (l, 0, j)),
        ],
        out_specs=pl.BlockSpec((None, b, tn), lambda l, j: (l, 0, j)),
        out_shape=jax.ShapeDtypeStruct((nl, b, n6), F32),
        compiler_params=_cparams(("arbitrary", "arbitrary")),
        name="ada_mod",
    )(c, ada_w, ada_b.reshape(nl, 1, n6))


_O_Z = 0
_O_XBC = _O_Z + SSD_WIDTH
_O_MQKV = _O_XBC + XBC_WIDTH
_O_SQKV = _O_MQKV + 3 * MOBA_WIDTH
_O_DT = _O_SQKV + SWA_WIDTH + 2 * SWA_KV_WIDTH


def _inproj_kernel(x_ref, mod_ref, g_ref, w_ref, z_ref, xbc_ref, dt_ref, mqkv_ref, sqkv_ref):
    h = _rms_rows(x_ref[...], g_ref[...]) * (1.0 + mod_ref[1:2, :]) + mod_ref[0:1, :]
    hb = h.astype(BF16)
    z_ref[...] = _dot(hb, w_ref[:, _O_Z:_O_XBC])
    xbc_ref[...] = _dot(hb, w_ref[:, _O_XBC:_O_MQKV])
    mqkv_ref[...] = _dot(hb, w_ref[:, _O_MQKV:_O_SQKV]).astype(BF16)
    sqkv_ref[...] = _dot(hb, w_ref[:, _O_SQKV:_O_DT]).astype(BF16)
    dt_ref[...] = _dot(hb, w_ref[:, _O_DT:N_IN_PAD])


def in_proj(x, mod, g, w, tm=512):
    b, s, d = x.shape
    tok = lambda n: pl.BlockSpec((None, tm, n), lambda bi, i: (bi, i, 0))
    return pl.pallas_call(
        _inproj_kernel,
        grid=(b, s // tm),
        in_specs=[
            tok(d),
            pl.BlockSpec((None, 6, d), lambda bi, i: (bi, 0, 0)),
            pl.BlockSpec((1, d), lambda bi, i: (0, 0)),
            pl.BlockSpec((d, N_IN_PAD), lambda bi, i: (0, 0)),
        ],
        out_specs=[tok(SSD_WIDTH), tok(XBC_WIDTH), tok(DT_PAD), tok(3 * MOBA_WIDTH),
                   tok(SWA_WIDTH + 2 * SWA_KV_WIDTH)],
        out_shape=[
            jax.ShapeDtypeStruct((b, s, SSD_WIDTH), F32),
            jax.ShapeDtypeStruct((b, s, XBC_WIDTH), F32),
            jax.ShapeDtypeStruct((b, s, DT_PAD), F32),
            jax.ShapeDtypeStruct((b, s, 3 * MOBA_WIDTH), BF16),
            jax.ShapeDtypeStruct((b, s, SWA_WIDTH + 2 * SWA_KV_WIDTH), BF16),
        ],
        compiler_params=_cparams(("arbitrary", "arbitrary")),
        name="in_proj",
    )(x, mod, g, w)


def _ssd_kernel(z_ref, xbc_ref, dt_ref, cw_ref, cb_ref, dtb_ref, alog_ref, dskip_ref, ng_ref,
                o_ref, xpad_ref, state_ref):
    L, N, P = SSD_CHUNK, SSD_STATE, HEAD_DIM
    GW = SSD_WIDTH // SSD_GROUPS
    c = pl.program_id(1)

    @pl.when(c == 0)
    def _():
        xpad_ref[0:8, :] = jnp.zeros((8, XBC_WIDTH), F32)
        state_ref[...] = jnp.zeros_like(state_ref)

    xpad_ref[8:8 + L, :] = xbc_ref[...]
    y = cb_ref[...] + cw_ref[0:1, :] * xpad_ref[5:5 + L, :]
    for k in range(1, CONV_K):
        y = y + cw_ref[k:k + 1, :] * xpad_ref[5 + k:5 + k + L, :]
    xpad_ref[0:8, :] = xpad_ref[L:L + 8, :]
    xc = _silu(y)
    xs = xc[:, :SSD_WIDTH]

    dtr = dt_ref[...] + dtb_ref[...]
    dtv = jnp.maximum(dtr, 0.0) + jnp.log1p(jnp.exp(-jnp.abs(dtr)))
    a = dtv * (-jnp.exp(alog_ref[...]))
    row = lax.broadcasted_iota(jnp.int32, (L, L), 0)
    col = lax.broadcasted_iota(jnp.int32, (L, L), 1)
    causal = col <= row
    a_cs = _dot(causal.astype(F32), a, HIGHEST)
    a_cs_t = a_cs.T
    er = lax.broadcasted_iota(jnp.int32, (DT_PAD, SSD_WIDTH), 0)
    ec = lax.broadcasted_iota(jnp.int32, (DT_PAD, SSD_WIDTH), 1)
    expand = (ec // P == er).astype(F32)
    dt_x = _dot(dtv, expand, HIGHEST)
    acs_x = _dot(a_cs, expand, HIGHEST)
    atot_x = acs_x[L - 1:L, :]

    xdt = xs * dt_x
    w_state = (xdt * jnp.exp(atot_x - acs_x)).astype(BF16)
    decay_in = jnp.exp(acs_x)
    decay_tot = jnp.exp(atot_x)
    lane_head = lax.broadcasted_iota(jnp.int32, (L, GW), 1) // P

    ys = []
    for g in range(SSD_GROUPS):
        bm = xc[:, SSD_WIDTH + g * N:SSD_WIDTH + (g + 1) * N].astype(BF16)
        cm = xc[:, SSD_WIDTH + SSD_GROUPS * N + g * N:SSD_WIDTH + SSD_GROUPS * N + (g + 1) * N].astype(BF16)
        cbm = _dot_nt(cm, bm)
        xdt_g = xdt[:, g * GW:(g + 1) * GW]
        y_g = jnp.zeros((L, GW), F32)
        for r in range(SSD_HEADS // SSD_GROUPS):
            hh = g * (SSD_HEADS // SSD_GROUPS) + r
            seg = jnp.broadcast_to(a_cs[:, hh:hh + 1], (L, L)) - jnp.broadcast_to(a_cs_t[hh:hh + 1, :], (L, L))
            decay = jnp.where(causal, jnp.exp(jnp.where(causal, seg, 0.0)), 0.0)
            m_h = (cbm * decay).astype(BF16)
            x_h = jnp.where(lane_head == r, xdt_g, 0.0).astype(BF16)
            y_g = y_g + _dot(m_h, x_h)
        st = state_ref[g]
        y_g = y_g + decay_in[:, g * GW:(g + 1) * GW] * _dot(cm, st.astype(BF16))
        state_ref[g] = decay_tot[:, g * GW:(g + 1) * GW] * st + _dot_tn(bm, w_state[:, g * GW:(g + 1) * GW])
        ys.append(y_g)
    yfull = jnp.concatenate(ys, axis=-1) + dskip_ref[...] * xs
    gated = yfull * _silu(z_ref[...])
    o_ref[...] = _rms_rows(gated, ng_ref[...]).astype(BF16)


def ssd_mixer(z, xbc, dt, conv_w, conv_b, dt_bias, a_log, d_skip_x, norm_g):
    b, s, _ = z.shape
    L = SSD_CHUNK
    tok = lambda n: pl.BlockSpec((None, L, n), lambda bi, i: (bi, i, 0))
    par = lambda r, n: pl.BlockSpec((r, n), lambda bi, i: (0, 0))
    return pl.pallas_call(
        _ssd_kernel,
        grid=(b, s // L),
        in_specs=[tok(SSD_WIDTH), tok(XBC_WIDTH), tok(DT_PAD), par(CONV_K, XBC_WIDTH), par(1, XBC_WIDTH),
                  par(1, DT_PAD), par(1, DT_PAD), par(1, SSD_WIDTH), par(1, SSD_WIDTH)],
        out_specs=tok(SSD_WIDTH),
        out_shape=jax.ShapeDtypeStruct((b, s, SSD_WIDTH), BF16),
        scratch_shapes=[pltpu.VMEM((L + 8, XBC_WIDTH), F32),
                        pltpu.VMEM((SSD_GROUPS, SSD_STATE, SSD_WIDTH // SSD_GROUPS), F32)],
        compiler_params=_cparams(("arbitrary", "arbitrary")),
        name="ssd_mixer",
    )(z, xbc, dt, conv_w, conv_b, dt_bias, a_log, d_skip_x, norm_g)


def _moba_kernel(q_ref, k_ref, v_ref, ng_ref, o_ref, kmean_ref, vt_ref, qs_ref, selb_ref, acc_ref):
    BL, Dh = MOBA_BLOCK, HEAD_DIM
    s = k_ref.shape[0]
    nb = s // BL
    i = pl.program_id(1)

    @pl.when(i == 0)
    def _():
        blk = lax.broadcasted_iota(jnp.int32, (nb, s), 0)
        pos = lax.broadcasted_iota(jnp.int32, (nb, s), 1)
        avg = jnp.where(pos // BL == blk, 1.0 / BL, 0.0).astype(BF16)
        kmean_ref[...] = _dot(avg, k_ref[...])
        for n in range(nb):
            vt_ref[:, n * BL:(n + 1) * BL] = v_ref[n * BL:(n + 1) * BL, :].astype(F32).T.astype(BF16)

    key_i = lax.broadcasted_iota(jnp.int32, (BL, BL), 0)
    qry_i = lax.broadcasted_iota(jnp.int32, (BL, BL), 1)
    own_bias = jnp.where(key_i <= qry_i, 0.0, MASKED)
    past = lax.broadcasted_iota(jnp.int32, (nb, BL), 0) < i
    heads = [slice(h * Dh, (h + 1) * Dh) for h in range(MOBA_HEADS)]

    qs_ref[...] = q_ref[...] * (Dh ** -0.5)
    for h, hs in enumerate(heads):
        gate = _dot_nt(kmean_ref[:, hs], qs_ref[:, hs].astype(F32), HIGHEST)
        cur = jnp.where(past, gate, -jnp.inf)
        for _ in range(MOBA_TOPK - 1):
            cur = jnp.where(cur == jnp.max(cur, axis=0, keepdims=True), -jnp.inf, cur)
        thr = jnp.max(cur, axis=0, keepdims=True)
        selb_ref[h] = jnp.where(past & (gate >= thr), 0.0, MASKED)

    def scores(rows, hs, bias, m):
        sc = _dot_nt(k_ref[rows, hs], qs_ref[:, hs]) + bias
        m_new = jnp.maximum(m, jnp.max(sc, axis=0, keepdims=True))
        p = jnp.exp(sc - m_new)
        return m_new, p, jnp.sum(p, axis=0, keepdims=True)

    own = pl.ds(pl.multiple_of(i * BL, BL), BL)
    ms, ls = [], []
    for hs in heads:
        m, p, psum = scores(own, hs, own_bias, jnp.full((1, BL), MASKED, F32))
        acc_ref[hs, :] = _dot(vt_ref[hs, own], p.astype(BF16))
        ms.append(m)
        ls.append(psum)

    def body(n, carry):
        rows = pl.ds(pl.multiple_of(n * BL, BL), BL)
        ms, ls = carry
        ms_new, ls_new = [], []
        for h, hs in enumerate(heads):
            m, p, psum = scores(rows, hs, selb_ref[h, pl.ds(n, 1), :], ms[h])
            alpha = jnp.exp(ms[h] - m)
            acc_ref[hs, :] = alpha * acc_ref[hs, :] + _dot(vt_ref[hs, rows], p.astype(BF16))
            ms_new.append(m)
            ls_new.append(alpha * ls[h] + psum)
        return tuple(ms_new), tuple(ls_new)

    ms, ls = lax.fori_loop(0, i, body, (tuple(ms), tuple(ls)))
    for h, hs in enumerate(heads):
        acc_ref[hs, :] = acc_ref[hs, :] / ls[h]
    o_ref[...] = _rms_rows(acc_ref[...].T, ng_ref[...]).astype(BF16)


def moba_attention(mqkv, norm_g):
    b, s, _ = mqkv.shape
    BL = MOBA_BLOCK
    nb = s // BL
    return pl.pallas_call(
        _moba_kernel,
        grid=(b, nb),
        in_specs=[
            pl.BlockSpec((None, BL, MOBA_WIDTH), lambda bi, i: (bi, i, 0)),
            pl.BlockSpec((None, s, MOBA_WIDTH), lambda bi, i: (bi, 0, 1)),
            pl.BlockSpec((None, s, MOBA_WIDTH), lambda bi, i: (bi, 0, 2)),
            pl.BlockSpec((1, MOBA_WIDTH), lambda bi, i: (0, 0)),
        ],
        out_specs=pl.BlockSpec((None, BL, MOBA_WIDTH), lambda bi, i: (bi, i, 0)),
        out_shape=jax.ShapeDtypeStruct((b, s, MOBA_WIDTH), BF16),
        scratch_shapes=[pltpu.VMEM((nb, MOBA_WIDTH), F32), pltpu.VMEM((MOBA_WIDTH, s), BF16),
                        pltpu.VMEM((BL, MOBA_WIDTH), BF16), pltpu.VMEM((MOBA_HEADS, nb, BL), F32),
                        pltpu.VMEM((MOBA_WIDTH, BL), F32)],
        compiler_params=_cparams(("arbitrary", "arbitrary")),
        name="moba_attention",
    )(mqkv, mqkv, mqkv, norm_g)


def _swa_kernel(sink_ref, q_ref, kp_ref, kc_ref, vp_ref, vc_ref, ng_ref, o_ref, out_t_ref):
    W, Dh = SWA_BLOCK, HEAD_DIM
    i = pl.program_id(1)
    key_i = lax.broadcasted_iota(jnp.int32, (2 * W, W), 0)
    qry_i = lax.broadcasted_iota(jnp.int32, (2 * W, W), 1)
    rel = qry_i + W - key_i
    ok = (rel >= 0) & (rel < W) & ((key_i >= W) | (i > 0))
    bias = jnp.where(ok, 0.0, MASKED)
    rep = SWA_HEADS // SWA_KV_HEADS
    for g in range(SWA_KV_HEADS):
        gs = slice(g * Dh, (g + 1) * Dh)
        k_band = jnp.concatenate([kp_ref[:, gs], kc_ref[:, gs]], axis=0)
        v_band = jnp.concatenate([vp_ref[:, gs], vc_ref[:, gs]], axis=0)
        for r in range(rep):
            hq = g * rep + r
            hs = slice(hq * Dh, (hq + 1) * Dh)
            q_h = q_ref[:, hs] * (Dh ** -0.5)
            sc = _dot_nt(k_band, q_h) + bias
            sink = sink_ref[hq]
            m = jnp.maximum(jnp.max(sc, axis=0, keepdims=True), sink)
            p = jnp.exp(sc - m)
            den = jnp.sum(p, axis=0, keepdims=True) + jnp.exp(sink - m)
            out_t_ref[hs, :] = _dot_tn(v_band, p.astype(BF16)) / den
    o_ref[...] = _rms_rows(out_t_ref[...].T, ng_ref[...]).astype(BF16)


def swa_attention(sqkv, sinks, norm_g):
    b, s, _ = sqkv.shape
    W = SWA_BLOCK
    prev = lambda bi, i: jnp.maximum(i - 1, 0)
    return pl.pallas_call(
        _swa_kernel,
        grid=(b, s // W),
        in_specs=[
            pl.BlockSpec(memory_space=pltpu.SMEM),
            pl.BlockSpec((None, W, SWA_WIDTH), lambda bi, i: (bi, i, 0)),
            pl.BlockSpec((None, W, SWA_KV_WIDTH), lambda bi, i: (bi, prev(bi, i), 2)),
            pl.BlockSpec((None, W, SWA_KV_WIDTH), lambda bi, i: (bi, i, 2)),
            pl.BlockSpec((None, W, SWA_KV_WIDTH), lambda bi, i: (bi, prev(bi, i), 3)),
            pl.BlockSpec((None, W, SWA_KV_WIDTH), lambda bi, i: (bi, i, 3)),
            pl.BlockSpec((1, SWA_WIDTH), lambda bi, i: (0, 0)),
        ],
        out_specs=pl.BlockSpec((None, W, SWA_WIDTH), lambda bi, i: (bi, i, 0)),
        out_shape=jax.ShapeDtypeStruct((b, s, SWA_WIDTH), BF16),
        scratch_shapes=[pltpu.VMEM((SWA_WIDTH, W), F32)],
        compiler_params=_cparams(("arbitrary", "arbitrary")),
        name="swa_attention",
    )(sinks, sqkv, sqkv, sqkv, sqkv, sqkv, norm_g)


def _outproj_kernel(x_ref, ys_ref, ym_ref, yw_ref, w_ref, mod_ref, g_ref, xo_ref, h_ref):
    ycat = jnp.concatenate([ys_ref[...], ym_ref[...], yw_ref[...]], axis=-1)
    xn = x_ref[...] + mod_ref[2:3, :] * _dot(ycat, w_ref[...])
    xo_ref[...] = xn
    h = _rms_rows(xn, g_ref[...]) * (1.0 + mod_ref[4:5, :]) + mod_ref[3:4, :]
    h_ref[...] = h.astype(BF16)


def out_proj(x, y_ssd, y_moba, y_swa, w_out, mod, g, tm=512):
    b, s, d = x.shape
    tok = lambda n: pl.BlockSpec((None, tm, n), lambda bi, i: (bi, i, 0))
    return pl.pallas_call(
        _outproj_kernel,
        grid=(b, s // tm),
        in_specs=[tok(d), tok(SSD_WIDTH), tok(MOBA_WIDTH), tok(SWA_WIDTH),
                  pl.BlockSpec((d, d), lambda bi, i: (0, 0)),
                  pl.BlockSpec((None, 6, d), lambda bi, i: (bi, 0, 0)),
                  pl.BlockSpec((1, d), lambda bi, i: (0, 0))],
        out_specs=[tok(d), tok(d)],
        out_shape=[jax.ShapeDtypeStruct((b, s, d), F32), jax.ShapeDtypeStruct((b, s, d), BF16)],
        compiler_params=_cparams(("arbitrary", "arbitrary")),
        name="out_proj",
    )(x, y_ssd, y_moba, y_swa, w_out, mod, g)


_CAND = [(r, c) for r in range(PEER_TOPK) for c in range(PEER_TOPK) if (r + 1) * (c + 1) <= PEER_TOPK]
_CAND_ROWS = -(-len(_CAND) // 8) * 8


def _top_rows(s_t, k, with_rank=False):
    rows = []
    cur = s_t
    rank = jnp.full(s_t.shape, float(k), F32)
    for r in range(k):
        m = jnp.max(cur, axis=0, keepdims=True)
        rows.append(m)
        hit = cur == m
        cur = jnp.where(hit, -jnp.inf, cur)
        if with_rank:
            rank = jnp.where(hit, float(r), rank)
    return (rows, rank) if with_rank else rows


def _route_kernel(h_ref, wq_ref, k1_ref, k2_ref, nl_ref, c1_ref, r2_ref, e2_ref, cand_ref):
    tm = h_ref.shape[0]
    q = _dot(h_ref[...], wq_ref[...])
    k1 = k1_ref[...]
    k2 = k2_ref[...]
    cand_ref[len(_CAND):, :] = jnp.full((_CAND_ROWS - len(_CAND), tm), -jnp.inf, F32)
    for h in range(PEER_HEADS):
        q1 = q[:, (2 * h) * PEER_HALF:(2 * h + 1) * PEER_HALF].astype(BF16)
        q2 = q[:, (2 * h + 1) * PEER_HALF:(2 * h + 2) * PEER_HALF].astype(BF16)
        s1 = _dot_nt(k1, q1)
        s2 = _dot_nt(k2, q2)
        v1 = _top_rows(s1, PEER_TOPK)
        v2, rank2 = _top_rows(s2, PEER_TOPK, with_rank=True)
        for idx, (r, c) in enumerate(_CAND):
            cand_ref[idx:idx + 1, :] = v1[r] + v2[c]
        cand = cand_ref[...]
        cur = cand
        for _ in range(PEER_TOPK - 1):
            cur = jnp.where(cur == jnp.max(cur, axis=0, keepdims=True), -jnp.inf, cur)
        theta = jnp.max(cur, axis=0, keepdims=True)
        top = v1[0] + v2[0]
        zsum = jnp.sum(jnp.where(cand >= theta, jnp.exp(cand - top), 0.0), axis=0, keepdims=True)
        nlim = jnp.zeros(s1.shape, F32)
        for c in range(PEER_TOPK):
            nlim = nlim + jnp.where(s1 + v2[c] >= theta, 1.0, 0.0)
        nl_ref[h] = nlim
        c1_ref[h] = jnp.exp(s1 - v1[0]) * (1.0 / zsum)
        r2_ref[h] = rank2.astype(BF16)
        e2_ref[h] = jnp.exp(s2 - v2[0]).astype(BF16)


def peer_route(h2, wq, k1, k2, tm=256):
    t, d = h2.shape
    big = lambda: pl.BlockSpec((PEER_HEADS, PEER_NKEYS, tm), lambda i: (0, 0, i))
    bigs = lambda dt: jax.ShapeDtypeStruct((PEER_HEADS, PEER_NKEYS, t), dt)
    return pl.pallas_call(
        _route_kernel,
        grid=(t // tm,),
        in_specs=[
            pl.BlockSpec((tm, d), lambda i: (i, 0)),
            pl.BlockSpec((d, 2 * PEER_HEADS * PEER_HALF), lambda i: (0, 0)),
            pl.BlockSpec((PEER_NKEYS, PEER_HALF), lambda i: (0, 0)),
            pl.BlockSpec((PEER_NKEYS, PEER_HALF), lambda i: (0, 0)),
        ],
        out_specs=[big(), big(), big(), big()],
        out_shape=[bigs(F32), bigs(F32), bigs(BF16), bigs(BF16)],
        scratch_shapes=[pltpu.VMEM((_CAND_ROWS, tm), F32)],
        compiler_params=_cparams(("arbitrary",)),
        name="peer_route",
    )(h2, wq, k1, k2)


_SLAB = 16
_LANES = 256
PEER_TM = 512
PEER_TE = 2048
PEER_SUB = 512


def _peer_kernel(h_ref, u_ref, v_ref, nl_ref, c1_ref, r2_ref, e2_ref, x_ref, mod_ref,
                 o_ref, pre_ref, act_ref, acc_ref):
    j = pl.program_id(1)
    tm = h_ref.shape[0]
    te = u_ref.shape[0]
    groups = PEER_SUB // PEER_NKEYS

    @pl.when(j == 0)
    def _():
        acc_ref[...] = jnp.zeros_like(acc_ref)

    hb = h_ref[...]
    for sc in range(te // PEER_SUB):
        srows = slice(sc * PEER_SUB, (sc + 1) * PEER_SUB)
        pre_ref[sc] = _dot_nt(u_ref[srows, :], hb)
        for gi in range(groups):
            i1 = j * (te // PEER_NKEYS) + sc * groups + gi
            for lt in range(tm // _LANES):
                lanes = slice(lt * _LANES, (lt + 1) * _LANES)
                nl_rows = [jnp.broadcast_to(nl_ref[h, pl.ds(i1, 1), lanes], (_SLAB, _LANES)).astype(BF16)
                           for h in range(PEER_HEADS)]
                c1_rows = [jnp.broadcast_to(c1_ref[h, pl.ds(i1, 1), lanes], (_SLAB, _LANES)).astype(BF16)
                           for h in range(PEER_HEADS)]
                for sl in range(PEER_NKEYS // _SLAB):
                    rows = slice(sl * _SLAB, (sl + 1) * _SLAB)
                    gsum = jnp.zeros((_SLAB, _LANES), BF16)
                    for h in range(PEER_HEADS):
                        hit = r2_ref[h, rows, lanes] < nl_rows[h]
                        gsum = gsum + jnp.where(hit, e2_ref[h, rows, lanes], jnp.zeros((), BF16)) * c1_rows[h]
                    prow = slice(gi * PEER_NKEYS + sl * _SLAB, gi * PEER_NKEYS + (sl + 1) * _SLAB)
                    pre = pre_ref[sc, prow, lanes]
                    gelu = 0.5 * pre * (1.0 + lax.erf(pre * (2.0 ** -0.5)))
                    act_ref[sc, prow, lanes] = gelu.astype(BF16) * gsum
        acc_ref[...] = acc_ref[...] + _dot_tn(act_ref[sc], v_ref[srows, :])

    @pl.when(j == pl.num_programs(1) - 1)
    def _():
        o_ref[...] = x_ref[...] + mod_ref[5:6, :] * acc_ref[...]


def peer_experts(h2, u, v, nl, c1, r2, e2, x, mod, seq, tm=PEER_TM, te=PEER_TE):
    t, d = h2.shape
    ne = u.shape[0]
    big = lambda: pl.BlockSpec((PEER_HEADS, PEER_NKEYS, tm), lambda i, j: (0, 0, i))
    nsub = te // PEER_SUB
    return pl.pallas_call(
        _peer_kernel,
        grid=(t // tm, ne // te),
        in_specs=[
            pl.BlockSpec((tm, d), lambda i, j: (i, 0)),
            pl.BlockSpec((te, d), lambda i, j: (j, 0)),
            pl.BlockSpec((te, d), lambda i, j: (j, 0)),
            big(), big(), big(), big(),
            pl.BlockSpec((tm, d), lambda i, j: (i, 0)),
            pl.BlockSpec((None, 6, d), lambda i, j: ((i * tm) // seq, 0, 0)),
        ],
        out_specs=pl.BlockSpec((tm, d), lambda i, j: (i, 0)),
        out_shape=jax.ShapeDtypeStruct((t, d), F32),
        scratch_shapes=[pltpu.VMEM((nsub, PEER_SUB, tm), F32), pltpu.VMEM((nsub, PEER_SUB, tm), BF16),
                        pltpu.VMEM((tm, d), F32)],
        compiler_params=_cparams(("arbitrary", "arbitrary")),
        name="peer_experts",
    )(h2, u, v, nl, c1, r2, e2, x, mod)


def _final_kernel(x_ref, g_ref, o_ref):
    o_ref[...] = _rms_rows(x_ref[...], g_ref[...])


def final_norm(x, g, tm=1024):
    t, d = x.shape
    return pl.pallas_call(
        _final_kernel,
        grid=(t // tm,),
        in_specs=[pl.BlockSpec((tm, d), lambda i: (i, 0)), pl.BlockSpec((1, d), lambda i: (0, 0))],
        out_specs=pl.BlockSpec((tm, d), lambda i: (i, 0)),
        out_shape=jax.ShapeDtypeStruct((t, d), F32),
        compiler_params=_cparams(("arbitrary",)),
        name="final_norm",
    )(x, g)


def _reorder_w_in(w_in):
    o_dt = SSD_WIDTH + XBC_WIDTH
    o_rest = o_dt + SSD_HEADS
    pad = jnp.zeros(w_in.shape[:-1] + (DT_PAD - SSD_HEADS,), w_in.dtype)
    return jnp.concatenate([w_in[..., :o_dt], w_in[..., o_rest:], w_in[..., o_dt:o_rest], pad], axis=-1)


def _pad_heads(p):
    return jnp.pad(p, ((0, 0), (0, DT_PAD - SSD_HEADS)))


def kernel(x, c, ada_w, ada_b, norm1_g, norm2_g, w_in, conv_w, conv_b, dt_bias, a_log, d_skip, ssd_norm_g, moba_norm_g, swa_sinks, swa_norm_g, w_out, peer_wq, peer_k1, peer_k2, peer_u, peer_v, final_g):
    b, s, d = x.shape
    depth = ada_w.shape[0]
    mod_all = ada_mod(c, ada_w, ada_b).reshape(depth, b, 6, d)
    w_in_b = _reorder_w_in(w_in).astype(BF16)
    w_out_b = w_out.astype(BF16)
    wq_b = peer_wq.astype(BF16)
    k1_b = peer_k1.astype(BF16)
    k2_b = peer_k2.astype(BF16)
    dtb = _pad_heads(dt_bias)
    alog = _pad_heads(a_log)
    dskip_x = jnp.repeat(d_skip, HEAD_DIM, axis=-1)
    for l in range(depth):
        mod = mod_all[l]
        z, xbc, dt, mqkv, sqkv = in_proj(x, mod, norm1_g[l][None], w_in_b[l])
        y_ssd = ssd_mixer(z, xbc, dt, conv_w[l], conv_b[l][None], dtb[l][None], alog[l][None],
                          dskip_x[l][None], ssd_norm_g[l][None])
        y_moba = moba_attention(mqkv, moba_norm_g[l][None])
        y_swa = swa_attention(sqkv, swa_sinks[l], swa_norm_g[l][None])
        x, h2 = out_proj(x, y_ssd, y_moba, y_swa, w_out_b[l], mod, norm2_g[l][None])
        h2f = h2.reshape(b * s, d)
        nl, c1, r2, e2 = peer_route(h2f, wq_b[l], k1_b[l], k2_b[l])
        x = peer_experts(h2f, peer_u[l].astype(BF16), peer_v[l].astype(BF16), nl, c1, r2, e2,
                         x.reshape(b * s, d), mod, s).reshape(b, s, d)
    return final_norm(x.reshape(b * s, d), final_g[None]).reshape(b, s, d)
```

```python
import functools

import numpy as np
import jax
import jax.numpy as jnp
from jax import lax
from jax.experimental import pallas as pl
from jax.experimental.pallas import tpu as pltpu

F32 = jnp.float32
BF16 = jnp.bfloat16
HIGHEST = lax.Precision.HIGHEST

EPS = 1e-5
D_MODEL = 1024
HEAD_DIM = 64
SSD_HEADS = 8
SSD_WIDTH = 512
SSD_GROUPS = 2
SSD_STATE = 128
SSD_CHUNK = 128
CONV_K = 4
XBC_WIDTH = 1024
MOBA_HEADS = 4
MOBA_WIDTH = 256
MOBA_BLOCK = 256
MOBA_TOPK = 3
SWA_HEADS = 4
SWA_KV_HEADS = 2
SWA_WIDTH = 256
SWA_KV_WIDTH = 128
SWA_BLOCK = 128
PEER_HEADS = 8
PEER_TOPK = 16
PEER_NKEYS = 128
PEER_EXPERTS = PEER_NKEYS * PEER_NKEYS
PEER_HALF = 128

DT_PAD = 128
N_IN_PAD = SSD_WIDTH + XBC_WIDTH + 3 * MOBA_WIDTH + SWA_WIDTH + 2 * SWA_KV_WIDTH + DT_PAD
MASKED = -1e30

VMEM_LIMIT = 56 * 1024 * 1024


def _cparams(sem):
    return pltpu.CompilerParams(dimension_semantics=sem, vmem_limit_bytes=VMEM_LIMIT)


def _dot(a, b, precision=None):
    return jnp.dot(a, b, preferred_element_type=F32, precision=precision)


def _dot_nt(a, b, precision=None):
    return lax.dot_general(a, b, (((1,), (1,)), ((), ())), preferred_element_type=F32,
                           precision=precision)


def _dot_tn(a, b, precision=None):
    return lax.dot_general(a, b, (((0,), (0,)), ((), ())), preferred_element_type=F32,
                           precision=precision)


def _silu(x):
    return x / (1.0 + jnp.exp(-x))


def _rms_rows(x, g):
    ms = jnp.mean(x * x, axis=-1, keepdims=True)
    return x * lax.rsqrt(ms + EPS) * g


def _ada_kernel(c_ref, w_ref, b_ref, o_ref):
    cond = _silu(c_ref[...])
    o_ref[...] = _dot(cond, w_ref[...], HIGHEST) + b_ref[...]


def ada_mod(c, ada_w, ada_b, tn=2048):
    nl, d, n6 = ada_w.shape
    b = c.shape[0]
    return pl.pallas_call(
        _ada_kernel,
        grid=(nl, n6 // tn),
        in_specs=[
            pl.BlockSpec((b, d), lambda l, j: (0, 0)),
            pl.BlockSpec((None, d, tn), lambda l, j: (l, 0, j)),
            pl.BlockSpec((None, 1, tn), lambda l, j: (l, 0, j)),
        ],
        out_specs=pl.BlockSpec((None, b, tn), lambda l, j: (l, 0, j)),
        out_shape=jax.ShapeDtypeStruct((nl, b, n6), F32),
        compiler_params=_cparams(("arbitrary", "arbitrary")),
        name="ada_mod",
    )(c, ada_w, ada_b.reshape(nl, 1, n6))


_O_Z = 0
_O_XBC = _O_Z + SSD_WIDTH
_O_MQKV = _O_XBC + XBC_WIDTH
_O_SQKV = _O_MQKV + 3 * MOBA_WIDTH
_O_DT = _O_SQKV + SWA_WIDTH + 2 * SWA_KV_WIDTH


def _inproj_kernel(x_ref, mod_ref, g_ref, w_ref, z_ref, xbc_ref, dt_ref, mqkv_ref, sqkv_ref):
    h = _rms_rows(x_ref[...], g_ref[...]) * (1.0 + mod_ref[1:2, :]) + mod_ref[0:1, :]
    hb = h.astype(BF16)
    z_ref[...] = _dot(hb, w_ref[:, _O_Z:_O_XBC])
    xbc_ref[...] = _dot(hb, w_ref[:, _O_XBC:_O_MQKV])
    mqkv_ref[...] = _dot(hb, w_ref[:, _O_MQKV:_O_SQKV]).astype(BF16)
    sqkv_ref[...] = _dot(hb, w_ref[:, _O_SQKV:_O_DT]).astype(BF16)
    dt_ref[...] = _dot(hb, w_ref[:, _O_DT:N_IN_PAD])


def in_proj(x, mod, g, w, tm=512):
    b, s, d = x.shape
    tok = lambda n: pl.BlockSpec((None, tm, n), lambda bi, i: (bi, i, 0))
    return pl.pallas_call(
        _inproj_kernel,
        grid=(b, s // tm),
        in_specs=[
            tok(d),
            pl.BlockSpec((None, 6, d), lambda bi, i: (bi, 0, 0)),
            pl.BlockSpec((1, d), lambda bi, i: (0, 0)),
            pl.BlockSpec((d, N_IN_PAD), lambda bi, i: (0, 0)),
        ],
        out_specs=[tok(SSD_WIDTH), tok(XBC_WIDTH), tok(DT_PAD), tok(3 * MOBA_WIDTH),
                   tok(SWA_WIDTH + 2 * SWA_KV_WIDTH)],
        out_shape=[
            jax.ShapeDtypeStruct((b, s, SSD_WIDTH), F32),
            jax.ShapeDtypeStruct((b, s, XBC_WIDTH), F32),
            jax.ShapeDtypeStruct((b, s, DT_PAD), F32),
            jax.ShapeDtypeStruct((b, s, 3 * MOBA_WIDTH), BF16),
            jax.ShapeDtypeStruct((b, s, SWA_WIDTH + 2 * SWA_KV_WIDTH), BF16),
        ],
        compiler_params=_cparams(("arbitrary", "arbitrary")),
        name="in_proj",
    )(x, mod, g, w)


def _ssd_kernel(z_ref, xbc_ref, dt_ref, cw_ref, cb_ref, dtb_ref, alog_ref, dskip_ref, ng_ref,
                o_ref, xpad_ref, state_ref):
    L, N, P = SSD_CHUNK, SSD_STATE, HEAD_DIM
    GW = SSD_WIDTH // SSD_GROUPS
    c = pl.program_id(1)

    @pl.when(c == 0)
    def _():
        xpad_ref[0:8, :] = jnp.zeros((8, XBC_WIDTH), F32)
        state_ref[...] = jnp.zeros_like(state_ref)

    xpad_ref[8:8 + L, :] = xbc_ref[...]
    y = cb_ref[...] + cw_ref[0:1, :] * xpad_ref[5:5 + L, :]
    for k in range(1, CONV_K):
        y = y + cw_ref[k:k + 1, :] * xpad_ref[5 + k:5 + k + L, :]
    xpad_ref[0:8, :] = xpad_ref[L:L + 8, :]
    xc = _silu(y)
    xs = xc[:, :SSD_WIDTH]

    dtr = dt_ref[...] + dtb_ref[...]
    dtv = jnp.maximum(dtr, 0.0) + jnp.log1p(jnp.exp(-jnp.abs(dtr)))
    a = dtv * (-jnp.exp(alog_ref[...]))
    row = lax.broadcasted_iota(jnp.int32, (L, L), 0)
    col = lax.broadcasted_iota(jnp.int32, (L, L), 1)
    causal = col <= row
    a_cs = _dot(causal.astype(F32), a, HIGHEST)
    a_cs_t = a_cs.T
    er = lax.broadcasted_iota(jnp.int32, (DT_PAD, SSD_WIDTH), 0)
    ec = lax.broadcasted_iota(jnp.int32, (DT_PAD, SSD_WIDTH), 1)
    expand = (ec // P == er).astype(F32)
    dt_x = _dot(dtv, expand, HIGHEST)
    acs_x = _dot(a_cs, expand, HIGHEST)
    atot_x = acs_x[L - 1:L, :]

    xdt = xs * dt_x
    w_state = (xdt * jnp.exp(atot_x - acs_x)).astype(BF16)
    decay_in = jnp.exp(acs_x)
    decay_tot = jnp.exp(atot_x)
    lane_head = lax.broadcasted_iota(jnp.int32, (L, GW), 1) // P

    ys = []
    for g in range(SSD_GROUPS):
        bm = xc[:, SSD_WIDTH + g * N:SSD_WIDTH + (g + 1) * N].astype(BF16)
        cm = xc[:, SSD_WIDTH + SSD_GROUPS * N + g * N:SSD_WIDTH + SSD_GROUPS * N + (g + 1) * N].astype(BF16)
        cbm = _dot_nt(cm, bm)
        xdt_g = xdt[:, g * GW:(g + 1) * GW]
        y_g = jnp.zeros((L, GW), F32)
        for r in range(SSD_HEADS // SSD_GROUPS):
            hh = g * (SSD_HEADS // SSD_GROUPS) + r
            seg = jnp.broadcast_to(a_cs[:, hh:hh + 1], (L, L)) - jnp.broadcast_to(a_cs_t[hh:hh + 1, :], (L, L))
            decay = jnp.where(causal, jnp.exp(jnp.where(causal, seg, 0.0)), 0.0)
            m_h = (cbm * decay).astype(BF16)
            x_h = jnp.where(lane_head == r, xdt_g, 0.0).astype(BF16)
            y_g = y_g + _dot(m_h, x_h)
        st = state_ref[g]
        y_g = y_g + decay_in[:, g * GW:(g + 1) * GW] * _dot(cm, st.astype(BF16))
        state_ref[g] = decay_tot[:, g * GW:(g + 1) * GW] * st + _dot_tn(bm, w_state[:, g * GW:(g + 1) * GW])
        ys.append(y_g)
    yfull = jnp.concatenate(ys, axis=-1) + dskip_ref[...] * xs
    gated = yfull * _silu(z_ref[...])
    o_ref[...] = _rms_rows(gated, ng_ref[...]).astype(BF16)


def ssd_mixer(z, xbc, dt, conv_w, conv_b, dt_bias, a_log, d_skip_x, norm_g):
    b, s, _ = z.shape
    L = SSD_CHUNK
    tok = lambda n: pl.BlockSpec((None, L, n), lambda bi, i: (bi, i, 0))
    par = lambda r, n: pl.BlockSpec((r, n), lambda bi, i: (0, 0))
    return pl.pallas_call(
        _ssd_kernel,
        grid=(b, s // L),
        in_specs=[tok(SSD_WIDTH), tok(XBC_WIDTH), tok(DT_PAD), par(CONV_K, XBC_WIDTH), par(1, XBC_WIDTH),
                  par(1, DT_PAD), par(1, DT_PAD), par(1, SSD_WIDTH), par(1, SSD_WIDTH)],
        out_specs=tok(SSD_WIDTH),
        out_shape=jax.ShapeDtypeStruct((b, s, SSD_WIDTH), BF16),
        scratch_shapes=[pltpu.VMEM((L + 8, XBC_WIDTH), F32),
                        pltpu.VMEM((SSD_GROUPS, SSD_STATE, SSD_WIDTH // SSD_GROUPS), F32)],
        compiler_params=_cparams(("arbitrary", "arbitrary")),
        name="ssd_mixer",
    )(z, xbc, dt, conv_w, conv_b, dt_bias, a_log, d_skip_x, norm_g)


def _moba_kernel(q_ref, k_ref, v_ref, ng_ref, o_ref, kmean_ref, vt_ref, qt_ref, selb_ref, sc_ref, acc_ref):
    BL, Dh = MOBA_BLOCK, HEAD_DIM
    s = k_ref.shape[0]
    nb = s // BL
    i = pl.program_id(1)

    @pl.when(i == 0)
    def _():
        blk = lax.broadcasted_iota(jnp.int32, (nb, s), 0)
        pos = lax.broadcasted_iota(jnp.int32, (nb, s), 1)
        avg = jnp.where(pos // BL == blk, 1.0 / BL, 0.0).astype(BF16)
        kmean_ref[...] = _dot(avg, k_ref[...])
        for n in range(nb):
            vt_ref[:, n * BL:(n + 1) * BL] = v_ref[n * BL:(n + 1) * BL, :].astype(F32).T.astype(BF16)

    key_i = lax.broadcasted_iota(jnp.int32, (BL, BL), 0)
    qry_i = lax.broadcasted_iota(jnp.int32, (BL, BL), 1)
    own_bias = jnp.where(key_i <= qry_i, 0.0, MASKED)
    past = lax.broadcasted_iota(jnp.int32, (nb, BL), 0) < i
    heads = [slice(h * Dh, (h + 1) * Dh) for h in range(MOBA_HEADS)]

    q_t = (q_ref[...] * (Dh ** -0.5)).astype(F32).T
    qt_ref[...] = q_t.astype(BF16)
    for h, hs in enumerate(heads):
        gate = _dot(kmean_ref[:, hs], q_t[hs, :], HIGHEST)
        cur = jnp.where(past, gate, -jnp.inf)
        for _ in range(MOBA_TOPK - 1):
            cur = jnp.where(cur == jnp.max(cur, axis=0, keepdims=True), -jnp.inf, cur)
        thr = jnp.max(cur, axis=0, keepdims=True)
        selb_ref[h] = jnp.where(past & (gate >= thr), 0.0, MASKED)

    def block(rows, biases, ms, ls, first):
        for h, hs in enumerate(heads):
            sc_ref[h] = _dot(k_ref[rows, hs], qt_ref[hs, :])
        ms_new, ls_new = [], []
        for h, hs in enumerate(heads):
            sc = sc_ref[h] + biases[h]
            m = jnp.maximum(ms[h], jnp.max(sc, axis=0, keepdims=True))
            p = jnp.exp(sc - m)
            psum = jnp.sum(p, axis=0, keepdims=True)
            pv = _dot(vt_ref[hs, rows], p.astype(BF16))
            if first:
                acc_ref[hs, :] = pv
                ls_new.append(psum)
            else:
                alpha = jnp.exp(ms[h] - m)
                acc_ref[hs, :] = alpha * acc_ref[hs, :] + pv
                ls_new.append(alpha * ls[h] + psum)
            ms_new.append(m)
        return tuple(ms_new), tuple(ls_new)

    own = pl.ds(pl.multiple_of(i * BL, BL), BL)
    init = tuple(jnp.full((1, BL), MASKED, F32) for _ in heads)
    carry = block(own, [own_bias] * MOBA_HEADS, init, None, True)

    def body(n, carry):
        rows = pl.ds(pl.multiple_of(n * BL, BL), BL)
        biases = [selb_ref[h, pl.ds(n, 1), :] for h in range(MOBA_HEADS)]
        return block(rows, biases, carry[0], carry[1], False)

    ms, ls = lax.fori_loop(0, i, body, carry)
    for h, hs in enumerate(heads):
        acc_ref[hs, :] = acc_ref[hs, :] / ls[h]
    o_ref[...] = _rms_rows(acc_ref[...].T, ng_ref[...]).astype(BF16)


def moba_attention(mqkv, norm_g):
    b, s, _ = mqkv.shape
    BL = MOBA_BLOCK
    nb = s // BL
    return pl.pallas_call(
        _moba_kernel,
        grid=(b, nb),
        in_specs=[
            pl.BlockSpec((None, BL, MOBA_WIDTH), lambda bi, i: (bi, i, 0)),
            pl.BlockSpec((None, s, MOBA_WIDTH), lambda bi, i: (bi, 0, 1)),
            pl.BlockSpec((None, s, MOBA_WIDTH), lambda bi, i: (bi, 0, 2)),
            pl.BlockSpec((1, MOBA_WIDTH), lambda bi, i: (0, 0)),
        ],
        out_specs=pl.BlockSpec((None, BL, MOBA_WIDTH), lambda bi, i: (bi, i, 0)),
        out_shape=jax.ShapeDtypeStruct((b, s, MOBA_WIDTH), BF16),
        scratch_shapes=[pltpu.VMEM((nb, MOBA_WIDTH), F32), pltpu.VMEM((MOBA_WIDTH, s), BF16),
                        pltpu.VMEM((MOBA_WIDTH, BL), BF16), pltpu.VMEM((MOBA_HEADS, nb, BL), F32),
                        pltpu.VMEM((MOBA_HEADS, BL, BL), F32), pltpu.VMEM((MOBA_WIDTH, BL), F32)],
        compiler_params=_cparams(("arbitrary", "arbitrary")),
        name="moba_attention",
    )(mqkv, mqkv, mqkv, norm_g)


def _swa_kernel(sink_ref, q_ref, kp_ref, kc_ref, vp_ref, vc_ref, ng_ref, o_ref, out_t_ref):
    W, Dh = SWA_BLOCK, HEAD_DIM
    i = pl.program_id(1)
    key_i = lax.broadcasted_iota(jnp.int32, (2 * W, W), 0)
    qry_i = lax.broadcasted_iota(jnp.int32, (2 * W, W), 1)
    rel = qry_i + W - key_i
    ok = (rel >= 0) & (rel < W) & ((key_i >= W) | (i > 0))
    bias = jnp.where(ok, 0.0, MASKED)
    rep = SWA_HEADS // SWA_KV_HEADS
    for g in range(SWA_KV_HEADS):
        gs = slice(g * Dh, (g + 1) * Dh)
        k_band = jnp.concatenate([kp_ref[:, gs], kc_ref[:, gs]], axis=0)
        v_band = jnp.concatenate([vp_ref[:, gs], vc_ref[:, gs]], axis=0)
        for r in range(rep):
            hq = g * rep + r
            hs = slice(hq * Dh, (hq + 1) * Dh)
            q_h = q_ref[:, hs] * (Dh ** -0.5)
            sc = _dot_nt(k_band, q_h) + bias
            sink = sink_ref[hq]
            m = jnp.maximum(jnp.max(sc, axis=0, keepdims=True), sink)
            p = jnp.exp(sc - m)
            den = jnp.sum(p, axis=0, keepdims=True) + jnp.exp(sink - m)
            out_t_ref[hs, :] = _dot_tn(v_band, p.astype(BF16)) / den
    o_ref[...] = _rms_rows(out_t_ref[...].T, ng_ref[...]).astype(BF16)


def swa_attention(sqkv, sinks, norm_g):
    b, s, _ = sqkv.shape
    W = SWA_BLOCK
    prev = lambda bi, i: jnp.maximum(i - 1, 0)
    return pl.pallas_call(
        _swa_kernel,
        grid=(b, s // W),
        in_specs=[
            pl.BlockSpec(memory_space=pltpu.SMEM),
            pl.BlockSpec((None, W, SWA_WIDTH), lambda bi, i: (bi, i, 0)),
            pl.BlockSpec((None, W, SWA_KV_WIDTH), lambda bi, i: (bi, prev(bi, i), 2)),
            pl.BlockSpec((None, W, SWA_KV_WIDTH), lambda bi, i: (bi, i, 2)),
            pl.BlockSpec((None, W, SWA_KV_WIDTH), lambda bi, i: (bi, prev(bi, i), 3)),
            pl.BlockSpec((None, W, SWA_KV_WIDTH), lambda bi, i: (bi, i, 3)),
            pl.BlockSpec((1, SWA_WIDTH), lambda bi, i: (0, 0)),
        ],
        out_specs=pl.BlockSpec((None, W, SWA_WIDTH), lambda bi, i: (bi, i, 0)),
        out_shape=jax.ShapeDtypeStruct((b, s, SWA_WIDTH), BF16),
        scratch_shapes=[pltpu.VMEM((SWA_WIDTH, W), F32)],
        compiler_params=_cparams(("arbitrary", "arbitrary")),
        name="swa_attention",
    )(sinks, sqkv, sqkv, sqkv, sqkv, sqkv, norm_g)


def _outproj_kernel(x_ref, ys_ref, ym_ref, yw_ref, w_ref, mod_ref, g_ref, xo_ref, h_ref, ht_ref):
    ycat = jnp.concatenate([ys_ref[...], ym_ref[...], yw_ref[...]], axis=-1)
    xn = x_ref[...] + mod_ref[2:3, :] * _dot(ycat, w_ref[...])
    xo_ref[...] = xn
    h = _rms_rows(xn, g_ref[...]) * (1.0 + mod_ref[4:5, :]) + mod_ref[3:4, :]
    h_ref[...] = h.astype(BF16)
    ht_ref[...] = h.T.astype(BF16)


def out_proj(x, y_ssd, y_moba, y_swa, w_out, mod, g, tm=512):
    b, s, d = x.shape
    nt = s // tm
    tok = lambda n: pl.BlockSpec((None, tm, n), lambda bi, i: (bi, i, 0))
    return pl.pallas_call(
        _outproj_kernel,
        grid=(b, nt),
        in_specs=[tok(d), tok(SSD_WIDTH), tok(MOBA_WIDTH), tok(SWA_WIDTH),
                  pl.BlockSpec((d, d), lambda bi, i: (0, 0)),
                  pl.BlockSpec((None, 6, d), lambda bi, i: (bi, 0, 0)),
                  pl.BlockSpec((1, d), lambda bi, i: (0, 0))],
        out_specs=[tok(d), tok(d), pl.BlockSpec((d, tm), lambda bi, i: (0, bi * nt + i))],
        out_shape=[jax.ShapeDtypeStruct((b, s, d), F32), jax.ShapeDtypeStruct((b, s, d), BF16),
                   jax.ShapeDtypeStruct((d, b * s), BF16)],
        compiler_params=_cparams(("arbitrary", "arbitrary")),
        name="out_proj",
    )(x, y_ssd, y_moba, y_swa, w_out, mod, g)


_CAND = [(r, c) for r in range(PEER_TOPK) for c in range(PEER_TOPK) if (r + 1) * (c + 1) <= PEER_TOPK]
_CAND_ROWS = -(-len(_CAND) // 8) * 8
_NL_SPLIT = 4
_NL_TOP = PEER_TOPK // (_NL_SPLIT + 1)


def _top_rows(s_t, k, with_rank=False):
    rows = []
    cur = s_t
    rank = jnp.full(s_t.shape, float(k), F32)
    for r in range(k):
        m = jnp.max(cur, axis=0, keepdims=True)
        rows.append(m)
        hit = cur == m
        cur = jnp.where(hit, -jnp.inf, cur)
        if with_rank:
            rank = jnp.where(hit, float(r), rank)
    return (rows, rank) if with_rank else rows


def _route_kernel(h_ref, wq_ref, k1_ref, k2_ref, nl_ref, c1_ref, r2_ref, e2_ref, cand_ref):
    tm = h_ref.shape[0]
    q = _dot(h_ref[...], wq_ref[...])
    k1 = k1_ref[...]
    k2 = k2_ref[...]
    cand_ref[len(_CAND):, :] = jnp.full((_CAND_ROWS - len(_CAND), tm), -jnp.inf, F32)
    for h in range(PEER_HEADS):
        q1 = q[:, (2 * h) * PEER_HALF:(2 * h + 1) * PEER_HALF].astype(BF16)
        q2 = q[:, (2 * h + 1) * PEER_HALF:(2 * h + 2) * PEER_HALF].astype(BF16)
        s1 = _dot_nt(k1, q1)
        s2 = _dot_nt(k2, q2)
        v1 = _top_rows(s1, PEER_TOPK)
        v2, rank2 = _top_rows(s2, PEER_TOPK, with_rank=True)
        for idx, (r, c) in enumerate(_CAND):
            cand_ref[idx:idx + 1, :] = v1[r] + v2[c]
        cand = cand_ref[...]
        cur = cand
        for _ in range(PEER_TOPK - 1):
            cur = jnp.where(cur == jnp.max(cur, axis=0, keepdims=True), -jnp.inf, cur)
        theta = jnp.max(cur, axis=0, keepdims=True)
        top = v1[0] + v2[0]
        zsum = jnp.sum(jnp.where(cand >= theta, jnp.exp(cand - top), 0.0), axis=0, keepdims=True)
        nlim = jnp.zeros(s1.shape, F32)
        for c in range(_NL_SPLIT):
            nlim = nlim + jnp.where(s1 + v2[c] >= theta, 1.0, 0.0)
        for r in range(_NL_TOP):
            tail = jnp.zeros_like(theta)
            for c in range(_NL_SPLIT, PEER_TOPK):
                if (r + 1) * (c + 1) <= PEER_TOPK:
                    tail = tail + jnp.where(v1[r] + v2[c] >= theta, 1.0, 0.0)
            nlim = nlim + jnp.where(s1 == v1[r], tail, 0.0)
        nl_ref[h] = nlim
        c1_ref[h] = jnp.exp(s1 - v1[0]) * (1.0 / zsum)
        r2_ref[h] = rank2.astype(BF16)
        e2_ref[h] = jnp.exp(s2 - v2[0]).astype(BF16)


def peer_route(h2, wq, k1, k2, tm=256):
    t, d = h2.shape
    big = lambda: pl.BlockSpec((PEER_HEADS, PEER_NKEYS, tm), lambda i: (0, 0, i))
    bigs = lambda dt: jax.ShapeDtypeStruct((PEER_HEADS, PEER_NKEYS, t), dt)
    return pl.pallas_call(
        _route_kernel,
        grid=(t // tm,),
        in_specs=[
            pl.BlockSpec((tm, d), lambda i: (i, 0)),
            pl.BlockSpec((d, 2 * PEER_HEADS * PEER_HALF), lambda i: (0, 0)),
            pl.BlockSpec((PEER_NKEYS, PEER_HALF), lambda i: (0, 0)),
            pl.BlockSpec((PEER_NKEYS, PEER_HALF), lambda i: (0, 0)),
        ],
        out_specs=[big(), big(), big(), big()],
        out_shape=[bigs(F32), bigs(F32), bigs(BF16), bigs(BF16)],
        scratch_shapes=[pltpu.VMEM((_CAND_ROWS, tm), F32)],
        compiler_params=_cparams(("arbitrary",)),
        name="peer_route",
    )(h2, wq, k1, k2)


_SLAB = 16
_LANES = 256
PEER_TM = 512
PEER_TE = 2048
PEER_SUB = 512


def _peer_kernel(ht_ref, u_ref, vt_ref, nl_ref, c1_ref, r2_ref, e2_ref, x_ref, mod_ref,
                 o_ref, pre_ref, act_ref, acc_ref):
    j = pl.program_id(1)
    tm = ht_ref.shape[1]
    te = u_ref.shape[0]
    groups = PEER_SUB // PEER_NKEYS

    @pl.when(j == 0)
    def _():
        acc_ref[...] = jnp.zeros_like(acc_ref)

    nsub = te // PEER_SUB
    sub_rows = [slice(sc * PEER_SUB, (sc + 1) * PEER_SUB) for sc in range(nsub)]
    pre_ref[0] = _dot(u_ref[sub_rows[0], :], ht_ref[...])
    for sc in range(nsub):
        srows = sub_rows[sc]
        if sc + 1 < nsub:
            pre_ref[sc + 1] = _dot(u_ref[sub_rows[sc + 1], :], ht_ref[...])
        for gi in range(groups):
            local = sc * groups + gi
            base8 = pl.multiple_of(j * (te // PEER_NKEYS) + (local // 8) * 8, 8)
            r8 = local % 8
            for lt in range(tm // _LANES):
                lanes = slice(lt * _LANES, (lt + 1) * _LANES)
                nl_rows = [jnp.broadcast_to(nl_ref[h, pl.ds(base8, 8), lanes][r8:r8 + 1, :],
                                            (_SLAB, _LANES)).astype(BF16) for h in range(PEER_HEADS)]
                c1_rows = [jnp.broadcast_to(c1_ref[h, pl.ds(base8, 8), lanes][r8:r8 + 1, :],
                                            (_SLAB, _LANES)).astype(BF16) for h in range(PEER_HEADS)]
                for sl in range(PEER_NKEYS // _SLAB):
                    rows = slice(sl * _SLAB, (sl + 1) * _SLAB)
                    gsum = jnp.zeros((_SLAB, _LANES), BF16)
                    for h in range(PEER_HEADS):
                        hit = r2_ref[h, rows, lanes] < nl_rows[h]
                        gsum = gsum + jnp.where(hit, e2_ref[h, rows, lanes], jnp.zeros((), BF16)) * c1_rows[h]
                    prow = slice(gi * PEER_NKEYS + sl * _SLAB, gi * PEER_NKEYS + (sl + 1) * _SLAB)
                    pre = pre_ref[sc, prow, lanes]
                    gelu = 0.5 * pre * (1.0 + lax.erf(pre * (2.0 ** -0.5)))
                    act_ref[sc, prow, lanes] = gelu.astype(BF16) * gsum
        acc_ref[...] = acc_ref[...] + _dot(vt_ref[:, srows], act_ref[sc])

    @pl.when(j == pl.num_programs(1) - 1)
    def _():
        o_ref[...] = x_ref[...] + mod_ref[5:6, :] * acc_ref[...].T


def peer_experts(h2t, u, vt, nl, c1, r2, e2, x, mod, seq, tm=PEER_TM, te=PEER_TE):
    d, t = h2t.shape
    ne = u.shape[0]
    big = lambda: pl.BlockSpec((PEER_HEADS, PEER_NKEYS, tm), lambda i, j: (0, 0, i))
    nsub = te // PEER_SUB
    return pl.pallas_call(
        _peer_kernel,
        grid=(t // tm, ne // te),
        in_specs=[
            pl.BlockSpec((d, tm), lambda i, j: (0, i)),
            pl.BlockSpec((te, d), lambda i, j: (j, 0)),
            pl.BlockSpec((d, te), lambda i, j: (0, j)),
            big(), big(), big(), big(),
            pl.BlockSpec((tm, d), lambda i, j: (i, 0)),
            pl.BlockSpec((None, 6, d), lambda i, j: ((i * tm) // seq, 0, 0)),
        ],
        out_specs=pl.BlockSpec((tm, d), lambda i, j: (i, 0)),
        out_shape=jax.ShapeDtypeStruct((t, d), F32),
        scratch_shapes=[pltpu.VMEM((nsub, PEER_SUB, tm), F32), pltpu.VMEM((nsub, PEER_SUB, tm), BF16),
                        pltpu.VMEM((d, tm), F32)],
        compiler_params=_cparams(("arbitrary", "arbitrary")),
        name="peer_experts",
    )(h2t, u, vt, nl, c1, r2, e2, x, mod)


def _final_kernel(x_ref, g_ref, o_ref):
    o_ref[...] = _rms_rows(x_ref[...], g_ref[...])


def final_norm(x, g, tm=1024):
    t, d = x.shape
    return pl.pallas_call(
        _final_kernel,
        grid=(t // tm,),
        in_specs=[pl.BlockSpec((tm, d), lambda i: (i, 0)), pl.BlockSpec((1, d), lambda i: (0, 0))],
        out_specs=pl.BlockSpec((tm, d), lambda i: (i, 0)),
        out_shape=jax.ShapeDtypeStruct((t, d), F32),
        compiler_params=_cparams(("arbitrary",)),
        name="final_norm",
    )(x, g)


def _reorder_w_in(w_in):
    o_dt = SSD_WIDTH + XBC_WIDTH
    o_rest = o_dt + SSD_HEADS
    pad = jnp.zeros(w_in.shape[:-1] + (DT_PAD - SSD_HEADS,), w_in.dtype)
    return jnp.concatenate([w_in[..., :o_dt], w_in[..., o_rest:], w_in[..., o_dt:o_rest], pad], axis=-1)


def _pad_heads(p):
    return jnp.pad(p, ((0, 0), (0, DT_PAD - SSD_HEADS)))


def kernel(x, c, ada_w, ada_b, norm1_g, norm2_g, w_in, conv_w, conv_b, dt_bias, a_log, d_skip, ssd_norm_g, moba_norm_g, swa_sinks, swa_norm_g, w_out, peer_wq, peer_k1, peer_k2, peer_u, peer_v, final_g):
    b, s, d = x.shape
    depth = ada_w.shape[0]
    mod_all = ada_mod(c, ada_w, ada_b).reshape(depth, b, 6, d)
    w_in_b = _reorder_w_in(w_in).astype(BF16)
    w_out_b = w_out.astype(BF16)
    wq_b = peer_wq.astype(BF16)
    k1_b = peer_k1.astype(BF16)
    k2_b = peer_k2.astype(BF16)
    dtb = _pad_heads(dt_bias)
    alog = _pad_heads(a_log)
    dskip_x = jnp.repeat(d_skip, HEAD_DIM, axis=-1)
    for l in range(depth):
        mod = mod_all[l]
        z, xbc, dt, mqkv, sqkv = in_proj(x, mod, norm1_g[l][None], w_in_b[l])
        y_ssd = ssd_mixer(z, xbc, dt, conv_w[l], conv_b[l][None], dtb[l][None], alog[l][None],
                          dskip_x[l][None], ssd_norm_g[l][None])
        y_moba = moba_attention(mqkv, moba_norm_g[l][None])
        y_swa = swa_attention(sqkv, swa_sinks[l], swa_norm_g[l][None])
        x, h2, h2t = out_proj(x, y_ssd, y_moba, y_swa, w_out_b[l], mod, norm2_g[l][None])
        nl, c1, r2, e2 = peer_route(h2.reshape(b * s, d), wq_b[l], k1_b[l], k2_b[l])
        x = peer_experts(h2t, peer_u[l].astype(BF16), peer_v[l].T.astype(BF16), nl, c1, r2, e2,
                         x.reshape(b * s, d), mod, s).reshape(b, s, d)
    return final_norm(x.reshape(b * s, d), final_g[None]).reshape(b, s, d)
```

```python
import functools

import numpy as np
import jax
import jax.numpy as jnp
from jax import lax
from jax.experimental import pallas as pl
from jax.experimental.pallas import tpu as pltpu

F32 = jnp.float32
BF16 = jnp.bfloat16
HIGHEST = lax.Precision.HIGHEST

EPS = 1e-5
D_MODEL = 1024
HEAD_DIM = 64
SSD_HEADS = 8
SSD_WIDTH = 512
SSD_GROUPS = 2
SSD_STATE = 128
SSD_CHUNK = 128
CONV_K = 4
XBC_WIDTH = 1024
MOBA_HEADS = 4
MOBA_WIDTH = 256
MOBA_BLOCK = 256
MOBA_TOPK = 3
SWA_HEADS = 4
SWA_KV_HEADS = 2
SWA_WIDTH = 256
SWA_KV_WIDTH = 128
SWA_BLOCK = 128
PEER_HEADS = 8
PEER_TOPK = 16
PEER_NKEYS = 128
PEER_EXPERTS = PEER_NKEYS * PEER_NKEYS
PEER_HALF = 128

DT_PAD = 128
N_IN_PAD = SSD_WIDTH + XBC_WIDTH + 3 * MOBA_WIDTH + SWA_WIDTH + 2 * SWA_KV_WIDTH + DT_PAD
MASKED = -1e30
LOG2E = 1.4426950408889634

VMEM_LIMIT = 56 * 1024 * 1024


def _cparams(sem):
    return pltpu.CompilerParams(dimension_semantics=sem, vmem_limit_bytes=VMEM_LIMIT)


def _dot(a, b, precision=None):
    return jnp.dot(a, b, preferred_element_type=F32, precision=precision)


def _dot_nt(a, b, precision=None):
    return lax.dot_general(a, b, (((1,), (1,)), ((), ())), preferred_element_type=F32,
                           precision=precision)


def _dot_tn(a, b, precision=None):
    return lax.dot_general(a, b, (((0,), (0,)), ((), ())), preferred_element_type=F32,
                           precision=precision)


def _silu(x):
    return x / (1.0 + jnp.exp(-x))


def _rms_rows(x, g):
    ms = jnp.mean(x * x, axis=-1, keepdims=True)
    return x * lax.rsqrt(ms + EPS) * g


def _ada_kernel(c_ref, w_ref, b_ref, o_ref):
    cond = _silu(c_ref[...])
    o_ref[...] = _dot(cond, w_ref[...], HIGHEST) + b_ref[...]


def ada_mod(c, ada_w, ada_b, tn=2048):
    nl, d, n6 = ada_w.shape
    b = c.shape[0]
    return pl.pallas_call(
        _ada_kernel,
        grid=(nl, n6 // tn),
        in_specs=[
            pl.BlockSpec((b, d), lambda l, j: (0, 0)),
            pl.BlockSpec((None, d, tn), lambda l, j: (l, 0, j)),
            pl.BlockSpec((None, 1, tn), lambda l, j: (l, 0, j)),
        ],
        out_specs=pl.BlockSpec((None, b, tn), lambda l, j: (l, 0, j)),
        out_shape=jax.ShapeDtypeStruct((nl, b, n6), F32),
        compiler_params=_cparams(("arbitrary", "arbitrary")),
        name="ada_mod",
    )(c, ada_w, ada_b.reshape(nl, 1, n6))


_O_Z = 0
_O_XBC = _O_Z + SSD_WIDTH
_O_MQKV = _O_XBC + XBC_WIDTH
_O_SQKV = _O_MQKV + 3 * MOBA_WIDTH
_O_DT = _O_SQKV + SWA_WIDTH + 2 * SWA_KV_WIDTH


def _inproj_kernel(x_ref, mod_ref, g_ref, w_ref, z_ref, xbc_ref, dt_ref, mqkv_ref, sqkv_ref):
    h = _rms_rows(x_ref[...], g_ref[...]) * (1.0 + mod_ref[1:2, :]) + mod_ref[0:1, :]
    hb = h.astype(BF16)
    z_ref[...] = _dot(hb, w_ref[:, _O_Z:_O_XBC])
    xbc_ref[...] = _dot(hb, w_ref[:, _O_XBC:_O_MQKV])
    mqkv_ref[...] = _dot(hb, w_ref[:, _O_MQKV:_O_SQKV]).astype(BF16)
    sqkv_ref[...] = _dot(hb, w_ref[:, _O_SQKV:_O_DT]).astype(BF16)
    dt_ref[...] = _dot(hb, w_ref[:, _O_DT:N_IN_PAD])


def in_proj(x, mod, g, w, tm=512):
    b, s, d = x.shape
    tok = lambda n: pl.BlockSpec((None, tm, n), lambda bi, i: (bi, i, 0))
    return pl.pallas_call(
        _inproj_kernel,
        grid=(b, s // tm),
        in_specs=[
            tok(d),
            pl.BlockSpec((None, 6, d), lambda bi, i: (bi, 0, 0)),
            pl.BlockSpec((1, d), lambda bi, i: (0, 0)),
            pl.BlockSpec((d, N_IN_PAD), lambda bi, i: (0, 0)),
        ],
        out_specs=[tok(SSD_WIDTH), tok(XBC_WIDTH), tok(DT_PAD), tok(3 * MOBA_WIDTH),
                   tok(SWA_WIDTH + 2 * SWA_KV_WIDTH)],
        out_shape=[
            jax.ShapeDtypeStruct((b, s, SSD_WIDTH), F32),
            jax.ShapeDtypeStruct((b, s, XBC_WIDTH), F32),
            jax.ShapeDtypeStruct((b, s, DT_PAD), F32),
            jax.ShapeDtypeStruct((b, s, 3 * MOBA_WIDTH), BF16),
            jax.ShapeDtypeStruct((b, s, SWA_WIDTH + 2 * SWA_KV_WIDTH), BF16),
        ],
        compiler_params=_cparams(("arbitrary", "arbitrary")),
        name="in_proj",
    )(x, mod, g, w)


def _ssd_kernel(z_ref, xbc_ref, dt_ref, cw_ref, cb_ref, dtb_ref, alog_ref, dskip_ref, ng_ref,
                o_ref, xpad_ref, state_ref):
    L, N, P = SSD_CHUNK, SSD_STATE, HEAD_DIM
    GW = SSD_WIDTH // SSD_GROUPS
    c = pl.program_id(1)

    @pl.when(c == 0)
    def _():
        xpad_ref[0:8, :] = jnp.zeros((8, XBC_WIDTH), F32)
        state_ref[...] = jnp.zeros_like(state_ref)

    xpad_ref[8:8 + L, :] = xbc_ref[...]
    y = cb_ref[...] + cw_ref[0:1, :] * xpad_ref[5:5 + L, :]
    for k in range(1, CONV_K):
        y = y + cw_ref[k:k + 1, :] * xpad_ref[5 + k:5 + k + L, :]
    xpad_ref[0:8, :] = xpad_ref[L:L + 8, :]
    xc = _silu(y)
    xs = xc[:, :SSD_WIDTH]

    dtr = dt_ref[...] + dtb_ref[...]
    dtv = jnp.maximum(dtr, 0.0) + jnp.log1p(jnp.exp(-jnp.abs(dtr)))
    a = dtv * (-jnp.exp(alog_ref[...]))
    row = lax.broadcasted_iota(jnp.int32, (L, L), 0)
    col = lax.broadcasted_iota(jnp.int32, (L, L), 1)
    causal = col <= row
    a_cs = _dot(causal.astype(F32), a, HIGHEST)
    a_cs_t = a_cs.T
    er = lax.broadcasted_iota(jnp.int32, (DT_PAD, SSD_WIDTH), 0)
    ec = lax.broadcasted_iota(jnp.int32, (DT_PAD, SSD_WIDTH), 1)
    expand = (ec // P == er).astype(F32)
    dt_x = _dot(dtv, expand, HIGHEST)
    acs_x = _dot(a_cs, expand, HIGHEST)
    atot_x = acs_x[L - 1:L, :]

    xdt = xs * dt_x
    w_state = (xdt * jnp.exp(atot_x - acs_x)).astype(BF16)
    decay_in = jnp.exp(acs_x)
    decay_tot = jnp.exp(atot_x)
    lane_head = lax.broadcasted_iota(jnp.int32, (L, GW), 1) // P

    ys = []
    for g in range(SSD_GROUPS):
        bm = xc[:, SSD_WIDTH + g * N:SSD_WIDTH + (g + 1) * N].astype(BF16)
        cm = xc[:, SSD_WIDTH + SSD_GROUPS * N + g * N:SSD_WIDTH + SSD_GROUPS * N + (g + 1) * N].astype(BF16)
        cbm = _dot_nt(cm, bm)
        xdt_g = xdt[:, g * GW:(g + 1) * GW]
        y_g = jnp.zeros((L, GW), F32)
        for r in range(SSD_HEADS // SSD_GROUPS):
            hh = g * (SSD_HEADS // SSD_GROUPS) + r
            seg = jnp.broadcast_to(a_cs[:, hh:hh + 1], (L, L)) - jnp.broadcast_to(a_cs_t[hh:hh + 1, :], (L, L))
            decay = jnp.where(causal, jnp.exp(jnp.where(causal, seg, 0.0)), 0.0)
            m_h = (cbm * decay).astype(BF16)
            x_h = jnp.where(lane_head == r, xdt_g, 0.0).astype(BF16)
            y_g = y_g + _dot(m_h, x_h)
        st = state_ref[g]
        y_g = y_g + decay_in[:, g * GW:(g + 1) * GW] * _dot(cm, st.astype(BF16))
        state_ref[g] = decay_tot[:, g * GW:(g + 1) * GW] * st + _dot_tn(bm, w_state[:, g * GW:(g + 1) * GW])
        ys.append(y_g)
    yfull = jnp.concatenate(ys, axis=-1) + dskip_ref[...] * xs
    gated = yfull * _silu(z_ref[...])
    o_ref[...] = _rms_rows(gated, ng_ref[...]).astype(BF16)


def ssd_mixer(z, xbc, dt, conv_w, conv_b, dt_bias, a_log, d_skip_x, norm_g):
    b, s, _ = z.shape
    L = SSD_CHUNK
    tok = lambda n: pl.BlockSpec((None, L, n), lambda bi, i: (bi, i, 0))
    par = lambda r, n: pl.BlockSpec((r, n), lambda bi, i: (0, 0))
    return pl.pallas_call(
        _ssd_kernel,
        grid=(b, s // L),
        in_specs=[tok(SSD_WIDTH), tok(XBC_WIDTH), tok(DT_PAD), par(CONV_K, XBC_WIDTH), par(1, XBC_WIDTH),
                  par(1, DT_PAD), par(1, DT_PAD), par(1, SSD_WIDTH), par(1, SSD_WIDTH)],
        out_specs=tok(SSD_WIDTH),
        out_shape=jax.ShapeDtypeStruct((b, s, SSD_WIDTH), BF16),
        scratch_shapes=[pltpu.VMEM((L + 8, XBC_WIDTH), F32),
                        pltpu.VMEM((SSD_GROUPS, SSD_STATE, SSD_WIDTH // SSD_GROUPS), F32)],
        compiler_params=_cparams(("arbitrary", "arbitrary")),
        name="ssd_mixer",
    )(z, xbc, dt, conv_w, conv_b, dt_bias, a_log, d_skip_x, norm_g)


def _moba_kernel(q_ref, k_ref, v_ref, ng_ref, o_ref, kmean_ref, vt_ref, qt_ref, selb_ref, sc_ref, p_ref,
                 acc_ref):
    BL, Dh = MOBA_BLOCK, HEAD_DIM
    s = k_ref.shape[0]
    nb = s // BL
    i = pl.program_id(1)

    @pl.when(i == 0)
    def _():
        blk = lax.broadcasted_iota(jnp.int32, (nb, s), 0)
        pos = lax.broadcasted_iota(jnp.int32, (nb, s), 1)
        avg = jnp.where(pos // BL == blk, 1.0 / BL, 0.0).astype(BF16)
        kmean_ref[...] = _dot(avg, k_ref[...])
        for n in range(nb):
            vt_ref[:, n * BL:(n + 1) * BL] = v_ref[n * BL:(n + 1) * BL, :].astype(F32).T.astype(BF16)

    key_i = lax.broadcasted_iota(jnp.int32, (BL, BL), 0)
    qry_i = lax.broadcasted_iota(jnp.int32, (BL, BL), 1)
    own_bias = jnp.where(key_i <= qry_i, 0.0, MASKED)
    past = lax.broadcasted_iota(jnp.int32, (nb, BL), 0) < i
    heads = [slice(h * Dh, (h + 1) * Dh) for h in range(MOBA_HEADS)]

    q_t = (q_ref[...].astype(F32) * (Dh ** -0.5 * LOG2E)).T
    qt_ref[...] = q_t.astype(BF16)
    for h, hs in enumerate(heads):
        gate = _dot(kmean_ref[:, hs], q_t[hs, :], HIGHEST)
        cur = jnp.where(past, gate, -jnp.inf)
        for _ in range(MOBA_TOPK - 1):
            cur = jnp.where(cur == jnp.max(cur, axis=0, keepdims=True), -jnp.inf, cur)
        thr = jnp.max(cur, axis=0, keepdims=True)
        selb_ref[h] = jnp.where(past & (gate >= thr), 0.0, MASKED)

    def chunk(start, nblk, biases, ms, ls, first):
        rows = pl.ds(pl.multiple_of(start, BL), nblk * BL)
        for h, hs in enumerate(heads):
            sc_ref[h, :nblk * BL, :] = _dot(k_ref[rows, hs], qt_ref[hs, :])
        ms_new, ls_new = [], []
        for h, hs in enumerate(heads):
            scs = []
            m = ms[h]
            for t in range(nblk):
                sc = sc_ref[h, t * BL:(t + 1) * BL, :]
                if biases[h][t].shape[0] == 1:
                    m = jnp.maximum(m, jnp.max(sc, axis=0, keepdims=True) + biases[h][t])
                else:
                    sc = sc + biases[h][t]
                    m = jnp.maximum(m, jnp.max(sc, axis=0, keepdims=True))
                scs.append(sc)
            psum = jnp.zeros((1, BL), F32)
            for t, sc in enumerate(scs):
                shift = m - biases[h][t] if biases[h][t].shape[0] == 1 else m
                p = jnp.exp2(sc - shift)
                psum = psum + jnp.sum(p, axis=0, keepdims=True)
                p_ref[h, t * BL:(t + 1) * BL, :] = p.astype(BF16)
            pv = _dot(vt_ref[hs, rows], p_ref[h, :nblk * BL, :])
            if first:
                acc_ref[hs, :] = pv
                ls_new.append(psum)
            else:
                alpha = jnp.exp2(ms[h] - m)
                acc_ref[hs, :] = alpha * acc_ref[hs, :] + pv
                ls_new.append(alpha * ls[h] + psum)
            ms_new.append(m)
        return tuple(ms_new), tuple(ls_new)

    init = tuple(jnp.full((1, BL), MASKED, F32) for _ in heads)

    def first_single():
        return chunk(i * BL, 1, [[own_bias]] * MOBA_HEADS, init, None, True)

    def first_pair():
        biases = [[selb_ref[h, pl.ds(i - 1, 1), :], own_bias] for h in range(MOBA_HEADS)]
        return chunk((i - 1) * BL, 2, biases, init, None, True)

    carry = lax.cond(i % 2 == 1, first_pair, first_single)

    def body(n, carry):
        biases = [[selb_ref[h, pl.ds(2 * n, 1), :], selb_ref[h, pl.ds(2 * n + 1, 1), :]]
                  for h in range(MOBA_HEADS)]
        return chunk(2 * n * BL, 2, biases, carry[0], carry[1], False)

    ms, ls = lax.fori_loop(0, i // 2, body, carry)
    for h, hs in enumerate(heads):
        acc_ref[hs, :] = acc_ref[hs, :] / ls[h]
    o_ref[...] = _rms_rows(acc_ref[...].T, ng_ref[...]).astype(BF16)


def moba_attention(mqkv, norm_g):
    b, s, _ = mqkv.shape
    BL = MOBA_BLOCK
    nb = s // BL
    return pl.pallas_call(
        _moba_kernel,
        grid=(b, nb),
        in_specs=[
            pl.BlockSpec((None, BL, MOBA_WIDTH), lambda bi, i: (bi, i, 0)),
            pl.BlockSpec((None, s, MOBA_WIDTH), lambda bi, i: (bi, 0, 1)),
            pl.BlockSpec((None, s, MOBA_WIDTH), lambda bi, i: (bi, 0, 2)),
            pl.BlockSpec((1, MOBA_WIDTH), lambda bi, i: (0, 0)),
        ],
        out_specs=pl.BlockSpec((None, BL, MOBA_WIDTH), lambda bi, i: (bi, i, 0)),
        out_shape=jax.ShapeDtypeStruct((b, s, MOBA_WIDTH), BF16),
        scratch_shapes=[pltpu.VMEM((nb, MOBA_WIDTH), F32), pltpu.VMEM((MOBA_WIDTH, s), BF16),
                        pltpu.VMEM((MOBA_WIDTH, BL), BF16), pltpu.VMEM((MOBA_HEADS, nb, BL), F32),
                        pltpu.VMEM((MOBA_HEADS, 2 * BL, BL), F32), pltpu.VMEM((MOBA_HEADS, 2 * BL, BL), BF16),
                        pltpu.VMEM((MOBA_WIDTH, BL), F32)],
        compiler_params=_cparams(("arbitrary", "arbitrary")),
        name="moba_attention",
    )(mqkv, mqkv, mqkv, norm_g)


def _swa_kernel(sink_ref, q_ref, kp_ref, kc_ref, vp_ref, vc_ref, ng_ref, o_ref, sc_ref, out_t_ref):
    W, Dh = SWA_BLOCK, HEAD_DIM
    i = pl.program_id(1)
    key_i = lax.broadcasted_iota(jnp.int32, (2 * W, W), 0)
    qry_i = lax.broadcasted_iota(jnp.int32, (2 * W, W), 1)
    rel = qry_i + W - key_i
    ok = (rel >= 0) & (rel < W) & ((key_i >= W) | (i > 0))
    bias = jnp.where(ok, 0.0, MASKED)
    rep = SWA_HEADS // SWA_KV_HEADS
    q_t = (q_ref[...].astype(F32) * (Dh ** -0.5 * LOG2E)).T.astype(BF16)
    v_bands = []
    for g in range(SWA_KV_HEADS):
        gs = slice(g * Dh, (g + 1) * Dh)
        k_band = jnp.concatenate([kp_ref[:, gs], kc_ref[:, gs]], axis=0)
        v_bands.append(jnp.concatenate([vp_ref[:, gs], vc_ref[:, gs]], axis=0))
        for r in range(rep):
            hq = g * rep + r
            sc_ref[hq] = _dot(k_band, q_t[hq * Dh:(hq + 1) * Dh, :])
    for hq in range(SWA_HEADS):
        hs = slice(hq * Dh, (hq + 1) * Dh)
        sc = sc_ref[hq] + bias
        sink = sink_ref[hq] * LOG2E
        m = jnp.maximum(jnp.max(sc, axis=0, keepdims=True), sink)
        p = jnp.exp2(sc - m)
        den = jnp.sum(p, axis=0, keepdims=True) + jnp.exp2(sink - m)
        out_t_ref[hs, :] = _dot_tn(v_bands[hq // rep], p.astype(BF16)) / den
    o_ref[...] = _rms_rows(out_t_ref[...].T, ng_ref[...]).astype(BF16)


def swa_attention(sqkv, sinks, norm_g):
    b, s, _ = sqkv.shape
    W = SWA_BLOCK
    prev = lambda bi, i: jnp.maximum(i - 1, 0)
    return pl.pallas_call(
        _swa_kernel,
        grid=(b, s // W),
        in_specs=[
            pl.BlockSpec(memory_space=pltpu.SMEM),
            pl.BlockSpec((None, W, SWA_WIDTH), lambda bi, i: (bi, i, 0)),
            pl.BlockSpec((None, W, SWA_KV_WIDTH), lambda bi, i: (bi, prev(bi, i), 2)),
            pl.BlockSpec((None, W, SWA_KV_WIDTH), lambda bi, i: (bi, i, 2)),
            pl.BlockSpec((None, W, SWA_KV_WIDTH), lambda bi, i: (bi, prev(bi, i), 3)),
            pl.BlockSpec((None, W, SWA_KV_WIDTH), lambda bi, i: (bi, i, 3)),
            pl.BlockSpec((1, SWA_WIDTH), lambda bi, i: (0, 0)),
        ],
        out_specs=pl.BlockSpec((None, W, SWA_WIDTH), lambda bi, i: (bi, i, 0)),
        out_shape=jax.ShapeDtypeStruct((b, s, SWA_WIDTH), BF16),
        scratch_shapes=[pltpu.VMEM((SWA_HEADS, 2 * W, W), F32), pltpu.VMEM((SWA_WIDTH, W), F32)],
        compiler_params=_cparams(("arbitrary", "arbitrary")),
        name="swa_attention",
    )(sinks, sqkv, sqkv, sqkv, sqkv, sqkv, norm_g)


def _outproj_kernel(x_ref, ys_ref, ym_ref, yw_ref, w_ref, mod_ref, g_ref, xo_ref, h_ref, ht_ref):
    ycat = jnp.concatenate([ys_ref[...], ym_ref[...], yw_ref[...]], axis=-1)
    xn = x_ref[...] + mod_ref[2:3, :] * _dot(ycat, w_ref[...])
    xo_ref[...] = xn
    h = _rms_rows(xn, g_ref[...]) * (1.0 + mod_ref[4:5, :]) + mod_ref[3:4, :]
    h_ref[...] = h.astype(BF16)
    ht_ref[...] = h.T.astype(BF16)


def out_proj(x, y_ssd, y_moba, y_swa, w_out, mod, g, tm=512):
    b, s, d = x.shape
    nt = s // tm
    tok = lambda n: pl.BlockSpec((None, tm, n), lambda bi, i: (bi, i, 0))
    return pl.pallas_call(
        _outproj_kernel,
        grid=(b, nt),
        in_specs=[tok(d), tok(SSD_WIDTH), tok(MOBA_WIDTH), tok(SWA_WIDTH),
                  pl.BlockSpec((d, d), lambda bi, i: (0, 0)),
                  pl.BlockSpec((None, 6, d), lambda bi, i: (bi, 0, 0)),
                  pl.BlockSpec((1, d), lambda bi, i: (0, 0))],
        out_specs=[tok(d), tok(d), pl.BlockSpec((d, tm), lambda bi, i: (0, bi * nt + i))],
        out_shape=[jax.ShapeDtypeStruct((b, s, d), F32), jax.ShapeDtypeStruct((b, s, d), BF16),
                   jax.ShapeDtypeStruct((d, b * s), BF16)],
        compiler_params=_cparams(("arbitrary", "arbitrary")),
        name="out_proj",
    )(x, y_ssd, y_moba, y_swa, w_out, mod, g)


_CAND = [(r, c) for r in range(PEER_TOPK) for c in range(PEER_TOPK) if (r + 1) * (c + 1) <= PEER_TOPK]
_CAND_ROWS = -(-len(_CAND) // 8) * 8
_NL_SPLIT = 4
_NL_TOP = PEER_TOPK // (_NL_SPLIT + 1)


def _top_rows(s_t, k, with_rank=False):
    rows = []
    cur = s_t
    rank = jnp.full(s_t.shape, float(k), F32)
    for r in range(k):
        m = jnp.max(cur, axis=0, keepdims=True)
        rows.append(m)
        hit = cur == m
        cur = jnp.where(hit, -jnp.inf, cur)
        if with_rank:
            rank = jnp.where(hit, float(r), rank)
    return (rows, rank) if with_rank else rows


def _route_kernel(h_ref, wq_ref, k1_ref, k2_ref, nl_ref, c1_ref, r2_ref, e2_ref, cand_ref):
    tm = h_ref.shape[0]
    q = _dot(h_ref[...], wq_ref[...])
    k1 = k1_ref[...]
    k2 = k2_ref[...]
    cand_ref[len(_CAND):, :] = jnp.full((_CAND_ROWS - len(_CAND), tm), -jnp.inf, F32)
    for h in range(PEER_HEADS):
        q1 = q[:, (2 * h) * PEER_HALF:(2 * h + 1) * PEER_HALF].astype(BF16)
        q2 = q[:, (2 * h + 1) * PEER_HALF:(2 * h + 2) * PEER_HALF].astype(BF16)
        s1 = _dot_nt(k1, q1)
        s2 = _dot_nt(k2, q2)
        v1 = _top_rows(s1, PEER_TOPK)
        v2, rank2 = _top_rows(s2, PEER_TOPK, with_rank=True)
        for idx, (r, c) in enumerate(_CAND):
            cand_ref[idx:idx + 1, :] = v1[r] + v2[c]
        cand = cand_ref[...]
        cur = cand
        for _ in range(PEER_TOPK - 1):
            cur = jnp.where(cur == jnp.max(cur, axis=0, keepdims=True), -jnp.inf, cur)
        theta = jnp.max(cur, axis=0, keepdims=True)
        top = v1[0] + v2[0]
        zsum = jnp.sum(jnp.where(cand >= theta, jnp.exp(cand - top), 0.0), axis=0, keepdims=True)
        nlim = jnp.zeros(s1.shape, F32)
        for c in range(_NL_SPLIT):
            nlim = nlim + jnp.where(s1 + v2[c] >= theta, 1.0, 0.0)
        for r in range(_NL_TOP):
            tail = jnp.zeros_like(theta)
            for c in range(_NL_SPLIT, PEER_TOPK):
                if (r + 1) * (c + 1) <= PEER_TOPK:
                    tail = tail + jnp.where(v1[r] + v2[c] >= theta, 1.0, 0.0)
            nlim = nlim + jnp.where(s1 == v1[r], tail, 0.0)
        nl_ref[h] = nlim
        c1_ref[h] = jnp.exp(s1 - v1[0]) * (0.5 / zsum)
        r2_ref[h] = rank2.astype(BF16)
        e2_ref[h] = jnp.exp(s2 - v2[0]).astype(BF16)


def peer_route(h2, wq, k1, k2, tm=256):
    t, d = h2.shape
    big = lambda: pl.BlockSpec((PEER_HEADS, PEER_NKEYS, tm), lambda i: (0, 0, i))
    bigs = lambda dt: jax.ShapeDtypeStruct((PEER_HEADS, PEER_NKEYS, t), dt)
    return pl.pallas_call(
        _route_kernel,
        grid=(t // tm,),
        in_specs=[
            pl.BlockSpec((tm, d), lambda i: (i, 0)),
            pl.BlockSpec((d, 2 * PEER_HEADS * PEER_HALF), lambda i: (0, 0)),
            pl.BlockSpec((PEER_NKEYS, PEER_HALF), lambda i: (0, 0)),
            pl.BlockSpec((PEER_NKEYS, PEER_HALF), lambda i: (0, 0)),
        ],
        out_specs=[big(), big(), big(), big()],
        out_shape=[bigs(F32), bigs(F32), bigs(BF16), bigs(BF16)],
        scratch_shapes=[pltpu.VMEM((_CAND_ROWS, tm), F32)],
        compiler_params=_cparams(("arbitrary",)),
        name="peer_route",
    )(h2, wq, k1, k2)


_SLAB = 16
_LANES = 256
PEER_TM = 512
PEER_TE = 2048
PEER_SUB = 512


def _peer_kernel(ht_ref, u_ref, vt_ref, nl_ref, c1_ref, r2_ref, e2_ref, x_ref, mod_ref,
                 o_ref, pre_ref, act_ref, acc_ref):
    j = pl.program_id(1)
    tm = ht_ref.shape[1]
    te = u_ref.shape[0]
    groups = PEER_SUB // PEER_NKEYS

    @pl.when(j == 0)
    def _():
        acc_ref[...] = jnp.zeros_like(acc_ref)

    nsub = te // PEER_SUB
    sub_rows = [slice(sc * PEER_SUB, (sc + 1) * PEER_SUB) for sc in range(nsub)]
    pre_ref[0] = _dot(u_ref[sub_rows[0], :], ht_ref[...])
    for sc in range(nsub):
        srows = sub_rows[sc]
        if sc + 1 < nsub:
            pre_ref[sc + 1] = _dot(u_ref[sub_rows[sc + 1], :], ht_ref[...])
        for gi in range(groups):
            local = sc * groups + gi
            base8 = pl.multiple_of(j * (te // PEER_NKEYS) + (local // 8) * 8, 8)
            r8 = local % 8
            for lt in range(tm // _LANES):
                lanes = slice(lt * _LANES, (lt + 1) * _LANES)
                nslab = PEER_NKEYS // _SLAB
                slabs = [slice(sl * _SLAB, (sl + 1) * _SLAB) for sl in range(nslab)]
                gsum = [jnp.zeros((_SLAB, _LANES), BF16)] * nslab
                for h in range(PEER_HEADS):
                    nl_row = jnp.broadcast_to(nl_ref[h, pl.ds(base8, 8), lanes][r8:r8 + 1, :],
                                              (_SLAB, _LANES)).astype(BF16)
                    c1_row = jnp.broadcast_to(c1_ref[h, pl.ds(base8, 8), lanes][r8:r8 + 1, :],
                                              (_SLAB, _LANES)).astype(BF16)
                    for sl, rows in enumerate(slabs):
                        hit = r2_ref[h, rows, lanes] < nl_row
                        gsum[sl] = gsum[sl] + jnp.where(hit, e2_ref[h, rows, lanes], jnp.zeros((), BF16)) * c1_row
                for sl in range(nslab):
                    prow = slice(gi * PEER_NKEYS + sl * _SLAB, gi * PEER_NKEYS + (sl + 1) * _SLAB)
                    pre = pre_ref[sc, prow, lanes]
                    gelu2 = pre * (1.0 + lax.erf(pre * (2.0 ** -0.5)))
                    act_ref[sc, prow, lanes] = gelu2.astype(BF16) * gsum[sl]
        acc_ref[...] = acc_ref[...] + _dot(vt_ref[:, srows], act_ref[sc])

    @pl.when(j == pl.num_programs(1) - 1)
    def _():
        o_ref[...] = x_ref[...] + mod_ref[5:6, :] * acc_ref[...].T


def peer_experts(h2t, u, vt, nl, c1, r2, e2, x, mod, seq, tm=PEER_TM, te=PEER_TE):
    d, t = h2t.shape
    ne = u.shape[0]
    big = lambda: pl.BlockSpec((PEER_HEADS, PEER_NKEYS, tm), lambda i, j: (0, 0, i))
    nsub = te // PEER_SUB
    return pl.pallas_call(
        _peer_kernel,
        grid=(t // tm, ne // te),
        in_specs=[
            pl.BlockSpec((d, tm), lambda i, j: (0, i)),
            pl.BlockSpec((te, d), lambda i, j: (j, 0)),
            pl.BlockSpec((d, te), lambda i, j: (0, j)),
            big(), big(), big(), big(),
            pl.BlockSpec((tm, d), lambda i, j: (i, 0)),
            pl.BlockSpec((None, 6, d), lambda i, j: ((i * tm) // seq, 0, 0)),
        ],
        out_specs=pl.BlockSpec((tm, d), lambda i, j: (i, 0)),
        out_shape=jax.ShapeDtypeStruct((t, d), F32),
        scratch_shapes=[pltpu.VMEM((nsub, PEER_SUB, tm), F32), pltpu.VMEM((nsub, PEER_SUB, tm), BF16),
                        pltpu.VMEM((d, tm), F32)],
        compiler_params=_cparams(("arbitrary", "arbitrary")),
        name="peer_experts",
    )(h2t, u, vt, nl, c1, r2, e2, x, mod)


def _final_kernel(x_ref, g_ref, o_ref):
    o_ref[...] = _rms_rows(x_ref[...], g_ref[...])


def final_norm(x, g, tm=1024):
    t, d = x.shape
    return pl.pallas_call(
        _final_kernel,
        grid=(t // tm,),
        in_specs=[pl.BlockSpec((tm, d), lambda i: (i, 0)), pl.BlockSpec((1, d), lambda i: (0, 0))],
        out_specs=pl.BlockSpec((tm, d), lambda i: (i, 0)),
        out_shape=jax.ShapeDtypeStruct((t, d), F32),
        compiler_params=_cparams(("arbitrary",)),
        name="final_norm",
    )(x, g)


def _reorder_w_in(w_in):
    o_dt = SSD_WIDTH + XBC_WIDTH
    o_rest = o_dt + SSD_HEADS
    pad = jnp.zeros(w_in.shape[:-1] + (DT_PAD - SSD_HEADS,), w_in.dtype)
    return jnp.concatenate([w_in[..., :o_dt], w_in[..., o_rest:], w_in[..., o_dt:o_rest], pad], axis=-1)


def _pad_heads(p):
    return jnp.pad(p, ((0, 0), (0, DT_PAD - SSD_HEADS)))


def kernel(x, c, ada_w, ada_b, norm1_g, norm2_g, w_in, conv_w, conv_b, dt_bias, a_log, d_skip, ssd_norm_g, moba_norm_g, swa_sinks, swa_norm_g, w_out, peer_wq, peer_k1, peer_k2, peer_u, peer_v, final_g):
    b, s, d = x.shape
    depth = ada_w.shape[0]
    mod_all = ada_mod(c, ada_w, ada_b).reshape(depth, b, 6, d)
    w_in_b = _reorder_w_in(w_in).astype(BF16)
    w_out_b = w_out.astype(BF16)
    wq_b = peer_wq.astype(BF16)
    k1_b = peer_k1.astype(BF16)
    k2_b = peer_k2.astype(BF16)
    dtb = _pad_heads(dt_bias)
    alog = _pad_heads(a_log)
    dskip_x = jnp.repeat(d_skip, HEAD_DIM, axis=-1)
    for l in range(depth):
        mod = mod_all[l]
        z, xbc, dt, mqkv, sqkv = in_proj(x, mod, norm1_g[l][None], w_in_b[l])
        y_ssd = ssd_mixer(z, xbc, dt, conv_w[l], conv_b[l][None], dtb[l][None], alog[l][None],
                          dskip_x[l][None], ssd_norm_g[l][None])
        y_moba = moba_attention(mqkv, moba_norm_g[l][None])
        y_swa = swa_attention(sqkv, swa_sinks[l], swa_norm_g[l][None])
        x, h2, h2t = out_proj(x, y_ssd, y_moba, y_swa, w_out_b[l], mod, norm2_g[l][None])
        nl, c1, r2, e2 = peer_route(h2.reshape(b * s, d), wq_b[l], k1_b[l], k2_b[l])
        x = peer_experts(h2t, peer_u[l].astype(BF16), peer_v[l].T.astype(BF16), nl, c1, r2, e2,
                         x.reshape(b * s, d), mod, s).reshape(b, s, d)
    return final_norm(x.reshape(b * s, d), final_g[None]).reshape(b, s, d)
```

```python
import functools

import numpy as np
import jax
import jax.numpy as jnp
from jax import lax
from jax.experimental import pallas as pl
from jax.experimental.pallas import tpu as pltpu

F32 = jnp.float32
BF16 = jnp.bfloat16
HIGHEST = lax.Precision.HIGHEST

EPS = 1e-5
D_MODEL = 1024
HEAD_DIM = 64
SSD_HEADS = 8
SSD_WIDTH = 512
SSD_GROUPS = 2
SSD_STATE = 128
SSD_CHUNK = 128
CONV_K = 4
XBC_WIDTH = 1024
MOBA_HEADS = 4
MOBA_WIDTH = 256
MOBA_BLOCK = 256
MOBA_TOPK = 3
SWA_HEADS = 4
SWA_KV_HEADS = 2
SWA_WIDTH = 256
SWA_KV_WIDTH = 128
SWA_BLOCK = 128
PEER_HEADS = 8
PEER_TOPK = 16
PEER_NKEYS = 128
PEER_EXPERTS = PEER_NKEYS * PEER_NKEYS
PEER_HALF = 128

DT_PAD = 128
N_IN_PAD = SSD_WIDTH + XBC_WIDTH + 3 * MOBA_WIDTH + SWA_WIDTH + 2 * SWA_KV_WIDTH + DT_PAD
MASKED = -1e30
LOG2E = 1.4426950408889634

VMEM_LIMIT = 56 * 1024 * 1024


def _cparams(sem):
    return pltpu.CompilerParams(dimension_semantics=sem, vmem_limit_bytes=VMEM_LIMIT)


def _dot(a, b, precision=None):
    return jnp.dot(a, b, preferred_element_type=F32, precision=precision)


def _dot_nt(a, b, precision=None):
    return lax.dot_general(a, b, (((1,), (1,)), ((), ())), preferred_element_type=F32,
                           precision=precision)


def _dot_tn(a, b, precision=None):
    return lax.dot_general(a, b, (((0,), (0,)), ((), ())), preferred_element_type=F32,
                           precision=precision)


def _silu(x):
    return x / (1.0 + jnp.exp(-x))


def _rms_rows(x, g):
    ms = jnp.mean(x * x, axis=-1, keepdims=True)
    return x * lax.rsqrt(ms + EPS) * g


def _ada_kernel(c_ref, w_ref, b_ref, o_ref):
    cond = _silu(c_ref[...])
    o_ref[...] = _dot(cond, w_ref[...], HIGHEST) + b_ref[...]


def ada_mod(c, ada_w, ada_b, tn=2048):
    nl, d, n6 = ada_w.shape
    b = c.shape[0]
    return pl.pallas_call(
        _ada_kernel,
        grid=(nl, n6 // tn),
        in_specs=[
            pl.BlockSpec((b, d), lambda l, j: (0, 0)),
            pl.BlockSpec((None, d, tn), lambda l, j: (l, 0, j)),
            pl.BlockSpec((None, 1, tn), lambda l, j: (l, 0, j)),
        ],
        out_specs=pl.BlockSpec((None, b, tn), lambda l, j: (l, 0, j)),
        out_shape=jax.ShapeDtypeStruct((nl, b, n6), F32),
        compiler_params=_cparams(("arbitrary", "arbitrary")),
        name="ada_mod",
    )(c, ada_w, ada_b.reshape(nl, 1, n6))


_O_Z = 0
_O_XBC = _O_Z + SSD_WIDTH
_O_MQKV = _O_XBC + XBC_WIDTH
_O_SQKV = _O_MQKV + 3 * MOBA_WIDTH
_O_DT = _O_SQKV + SWA_WIDTH + 2 * SWA_KV_WIDTH


def _inproj_kernel(x_ref, mod_ref, g_ref, w_ref, z_ref, xbc_ref, dt_ref, mqkv_ref, sqkv_ref):
    h = _rms_rows(x_ref[...], g_ref[...]) * (1.0 + mod_ref[1:2, :]) + mod_ref[0:1, :]
    hb = h.astype(BF16)
    z_ref[...] = _dot(hb, w_ref[:, _O_Z:_O_XBC])
    xbc_ref[...] = _dot(hb, w_ref[:, _O_XBC:_O_MQKV])
    mqkv_ref[...] = _dot(hb, w_ref[:, _O_MQKV:_O_SQKV]).astype(BF16)
    sqkv_ref[...] = _dot(hb, w_ref[:, _O_SQKV:_O_DT]).astype(BF16)
    dt_ref[...] = _dot(hb, w_ref[:, _O_DT:N_IN_PAD])


def in_proj(x, mod, g, w, tm=512):
    b, s, d = x.shape
    tok = lambda n: pl.BlockSpec((None, tm, n), lambda bi, i: (bi, i, 0))
    return pl.pallas_call(
        _inproj_kernel,
        grid=(b, s // tm),
        in_specs=[
            tok(d),
            pl.BlockSpec((None, 6, d), lambda bi, i: (bi, 0, 0)),
            pl.BlockSpec((1, d), lambda bi, i: (0, 0)),
            pl.BlockSpec((d, N_IN_PAD), lambda bi, i: (0, 0)),
        ],
        out_specs=[tok(SSD_WIDTH), tok(XBC_WIDTH), tok(DT_PAD), tok(3 * MOBA_WIDTH),
                   tok(SWA_WIDTH + 2 * SWA_KV_WIDTH)],
        out_shape=[
            jax.ShapeDtypeStruct((b, s, SSD_WIDTH), F32),
            jax.ShapeDtypeStruct((b, s, XBC_WIDTH), F32),
            jax.ShapeDtypeStruct((b, s, DT_PAD), F32),
            jax.ShapeDtypeStruct((b, s, 3 * MOBA_WIDTH), BF16),
            jax.ShapeDtypeStruct((b, s, SWA_WIDTH + 2 * SWA_KV_WIDTH), BF16),
        ],
        compiler_params=_cparams(("arbitrary", "arbitrary")),
        name="in_proj",
    )(x, mod, g, w)


def _ssd_kernel(z_ref, xbc_ref, dt_ref, cw_ref, cb_ref, dtb_ref, alog_ref, dskip_ref, ng_ref,
                o_ref, xpad_ref, state_ref):
    L, N, P = SSD_CHUNK, SSD_STATE, HEAD_DIM
    GW = SSD_WIDTH // SSD_GROUPS
    c = pl.program_id(1)

    @pl.when(c == 0)
    def _():
        xpad_ref[0:8, :] = jnp.zeros((8, XBC_WIDTH), F32)
        state_ref[...] = jnp.zeros_like(state_ref)

    xpad_ref[8:8 + L, :] = xbc_ref[...]
    y = cb_ref[...] + cw_ref[0:1, :] * xpad_ref[5:5 + L, :]
    for k in range(1, CONV_K):
        y = y + cw_ref[k:k + 1, :] * xpad_ref[5 + k:5 + k + L, :]
    xpad_ref[0:8, :] = xpad_ref[L:L + 8, :]
    xc = _silu(y)
    xs = xc[:, :SSD_WIDTH]

    dtr = dt_ref[...] + dtb_ref[...]
    dtv = jnp.maximum(dtr, 0.0) + jnp.log1p(jnp.exp(-jnp.abs(dtr)))
    a = dtv * (-jnp.exp(alog_ref[...]))
    row = lax.broadcasted_iota(jnp.int32, (L, L), 0)
    col = lax.broadcasted_iota(jnp.int32, (L, L), 1)
    causal = col <= row
    a_cs = _dot(causal.astype(F32), a, HIGHEST)
    a_cs_t = a_cs.T
    er = lax.broadcasted_iota(jnp.int32, (DT_PAD, SSD_WIDTH), 0)
    ec = lax.broadcasted_iota(jnp.int32, (DT_PAD, SSD_WIDTH), 1)
    expand = (ec // P == er).astype(F32)
    dt_x = _dot(dtv, expand, HIGHEST)
    acs_x = _dot(a_cs, expand, HIGHEST)
    atot_x = acs_x[L - 1:L, :]

    xdt = xs * dt_x
    w_state = (xdt * jnp.exp(atot_x - acs_x)).astype(BF16)
    decay_in = jnp.exp(acs_x)
    decay_tot = jnp.exp(atot_x)
    lane_head = lax.broadcasted_iota(jnp.int32, (L, GW), 1) // P

    ys = []
    for g in range(SSD_GROUPS):
        bm = xc[:, SSD_WIDTH + g * N:SSD_WIDTH + (g + 1) * N].astype(BF16)
        cm = xc[:, SSD_WIDTH + SSD_GROUPS * N + g * N:SSD_WIDTH + SSD_GROUPS * N + (g + 1) * N].astype(BF16)
        cbm = _dot_nt(cm, bm)
        xdt_g = xdt[:, g * GW:(g + 1) * GW]
        y_g = jnp.zeros((L, GW), F32)
        for r in range(SSD_HEADS // SSD_GROUPS):
            hh = g * (SSD_HEADS // SSD_GROUPS) + r
            seg = jnp.broadcast_to(a_cs[:, hh:hh + 1], (L, L)) - jnp.broadcast_to(a_cs_t[hh:hh + 1, :], (L, L))
            decay = jnp.where(causal, jnp.exp(jnp.where(causal, seg, 0.0)), 0.0)
            m_h = (cbm * decay).astype(BF16)
            x_h = jnp.where(lane_head == r, xdt_g, 0.0).astype(BF16)
            y_g = y_g + _dot(m_h, x_h)
        st = state_ref[g]
        y_g = y_g + decay_in[:, g * GW:(g + 1) * GW] * _dot(cm, st.astype(BF16))
        state_ref[g] = decay_tot[:, g * GW:(g + 1) * GW] * st + _dot_tn(bm, w_state[:, g * GW:(g + 1) * GW])
        ys.append(y_g)
    yfull = jnp.concatenate(ys, axis=-1) + dskip_ref[...] * xs
    gated = yfull * _silu(z_ref[...])
    o_ref[...] = _rms_rows(gated, ng_ref[...]).astype(BF16)


def ssd_mixer(z, xbc, dt, conv_w, conv_b, dt_bias, a_log, d_skip_x, norm_g):
    b, s, _ = z.shape
    L = SSD_CHUNK
    tok = lambda n: pl.BlockSpec((None, L, n), lambda bi, i: (bi, i, 0))
    par = lambda r, n: pl.BlockSpec((r, n), lambda bi, i: (0, 0))
    return pl.pallas_call(
        _ssd_kernel,
        grid=(b, s // L),
        in_specs=[tok(SSD_WIDTH), tok(XBC_WIDTH), tok(DT_PAD), par(CONV_K, XBC_WIDTH), par(1, XBC_WIDTH),
                  par(1, DT_PAD), par(1, DT_PAD), par(1, SSD_WIDTH), par(1, SSD_WIDTH)],
        out_specs=tok(SSD_WIDTH),
        out_shape=jax.ShapeDtypeStruct((b, s, SSD_WIDTH), BF16),
        scratch_shapes=[pltpu.VMEM((L + 8, XBC_WIDTH), F32),
                        pltpu.VMEM((SSD_GROUPS, SSD_STATE, SSD_WIDTH // SSD_GROUPS), F32)],
        compiler_params=_cparams(("arbitrary", "arbitrary")),
        name="ssd_mixer",
    )(z, xbc, dt, conv_w, conv_b, dt_bias, a_log, d_skip_x, norm_g)


def _moba_kernel(q_ref, k_ref, v_ref, ng_ref, o_ref, kmean_ref, vt_ref, qt_ref, selb_ref, sc_ref, p_ref,
                 acc_ref):
    BL, Dh = MOBA_BLOCK, HEAD_DIM
    s = k_ref.shape[0]
    nb = s // BL
    i = pl.program_id(1)

    @pl.when(i == 0)
    def _():
        blk = lax.broadcasted_iota(jnp.int32, (nb, s), 0)
        pos = lax.broadcasted_iota(jnp.int32, (nb, s), 1)
        avg = jnp.where(pos // BL == blk, 1.0 / BL, 0.0).astype(BF16)
        kmean_ref[...] = _dot(avg, k_ref[...])
        for n in range(nb):
            vt_ref[:, n * BL:(n + 1) * BL] = v_ref[n * BL:(n + 1) * BL, :].astype(F32).T.astype(BF16)

    key_i = lax.broadcasted_iota(jnp.int32, (BL, BL), 0)
    qry_i = lax.broadcasted_iota(jnp.int32, (BL, BL), 1)
    own_bias = jnp.where(key_i <= qry_i, 0.0, MASKED)
    past = lax.broadcasted_iota(jnp.int32, (nb, BL), 0) < i
    heads = [slice(h * Dh, (h + 1) * Dh) for h in range(MOBA_HEADS)]

    q_t = (q_ref[...].astype(F32) * (Dh ** -0.5 * LOG2E)).T
    qt_ref[...] = q_t.astype(BF16)
    for h, hs in enumerate(heads):
        gate = _dot(kmean_ref[:, hs], q_t[hs, :], HIGHEST)
        cur = jnp.where(past, gate, -jnp.inf)
        for _ in range(MOBA_TOPK - 1):
            cur = jnp.where(cur == jnp.max(cur, axis=0, keepdims=True), -jnp.inf, cur)
        thr = jnp.max(cur, axis=0, keepdims=True)
        selb_ref[h] = jnp.where(past & (gate >= thr), 0.0, MASKED)

    def chunk(start, nblk, biases, ms, ls, first):
        rows = pl.ds(pl.multiple_of(start, BL), nblk * BL)
        for h, hs in enumerate(heads):
            sc_ref[h, :nblk * BL, :] = _dot(k_ref[rows, hs], qt_ref[hs, :])
        ms_new, ls_new, pending = [], [], []
        for h, hs in enumerate(heads):
            scs = []
            m = ms[h]
            for t in range(nblk):
                sc = sc_ref[h, t * BL:(t + 1) * BL, :]
                if biases[h][t].shape[0] == 1:
                    m = jnp.maximum(m, jnp.max(sc, axis=0, keepdims=True) + biases[h][t])
                else:
                    sc = sc + biases[h][t]
                    m = jnp.maximum(m, jnp.max(sc, axis=0, keepdims=True))
                scs.append(sc)
            psum = jnp.zeros((1, BL), F32)
            for t, sc in enumerate(scs):
                shift = m - biases[h][t] if biases[h][t].shape[0] == 1 else m
                p = jnp.exp2(sc - shift)
                psum = psum + jnp.sum(p, axis=0, keepdims=True)
                p_ref[h, t * BL:(t + 1) * BL, :] = p.astype(BF16)
            pv = _dot(vt_ref[hs, rows], p_ref[h, :nblk * BL, :])
            alpha = None if first else jnp.exp2(ms[h] - m)
            ls_new.append(psum if first else alpha * ls[h] + psum)
            ms_new.append(m)
            for hsp, pvp, alphap in pending:
                acc_ref[hsp, :] = pvp if first else alphap * acc_ref[hsp, :] + pvp
            pending = [(hs, pv, alpha)]
        for hsp, pvp, alphap in pending:
            acc_ref[hsp, :] = pvp if first else alphap * acc_ref[hsp, :] + pvp
        return tuple(ms_new), tuple(ls_new)

    init = tuple(jnp.full((1, BL), MASKED, F32) for _ in heads)

    def first_single():
        return chunk(i * BL, 1, [[own_bias]] * MOBA_HEADS, init, None, True)

    def first_pair():
        biases = [[selb_ref[h, pl.ds(i - 1, 1), :], own_bias] for h in range(MOBA_HEADS)]
        return chunk((i - 1) * BL, 2, biases, init, None, True)

    carry = lax.cond(i % 2 == 1, first_pair, first_single)

    def body(n, carry):
        biases = [[selb_ref[h, pl.ds(2 * n, 1), :], selb_ref[h, pl.ds(2 * n + 1, 1), :]]
                  for h in range(MOBA_HEADS)]
        return chunk(2 * n * BL, 2, biases, carry[0], carry[1], False)

    ms, ls = lax.fori_loop(0, i // 2, body, carry)
    for h, hs in enumerate(heads):
        acc_ref[hs, :] = acc_ref[hs, :] / ls[h]
    o_ref[...] = _rms_rows(acc_ref[...].T, ng_ref[...]).astype(BF16)


def moba_attention(mqkv, norm_g):
    b, s, _ = mqkv.shape
    BL = MOBA_BLOCK
    nb = s // BL
    return pl.pallas_call(
        _moba_kernel,
        grid=(b, nb),
        in_specs=[
            pl.BlockSpec((None, BL, MOBA_WIDTH), lambda bi, i: (bi, i, 0)),
            pl.BlockSpec((None, s, MOBA_WIDTH), lambda bi, i: (bi, 0, 1)),
            pl.BlockSpec((None, s, MOBA_WIDTH), lambda bi, i: (bi, 0, 2)),
            pl.BlockSpec((1, MOBA_WIDTH), lambda bi, i: (0, 0)),
        ],
        out_specs=pl.BlockSpec((None, BL, MOBA_WIDTH), lambda bi, i: (bi, i, 0)),
        out_shape=jax.ShapeDtypeStruct((b, s, MOBA_WIDTH), BF16),
        scratch_shapes=[pltpu.VMEM((nb, MOBA_WIDTH), F32), pltpu.VMEM((MOBA_WIDTH, s), BF16),
                        pltpu.VMEM((MOBA_WIDTH, BL), BF16), pltpu.VMEM((MOBA_HEADS, nb, BL), F32),
                        pltpu.VMEM((MOBA_HEADS, 2 * BL, BL), F32), pltpu.VMEM((MOBA_HEADS, 2 * BL, BL), BF16),
                        pltpu.VMEM((MOBA_WIDTH, BL), F32)],
        compiler_params=_cparams(("arbitrary", "arbitrary")),
        name="moba_attention",
    )(mqkv, mqkv, mqkv, norm_g)


def _swa_kernel(sink_ref, q_ref, kp_ref, kc_ref, vp_ref, vc_ref, ng_ref, o_ref, sc_ref, out_t_ref):
    W, Dh = SWA_BLOCK, HEAD_DIM
    i = pl.program_id(1)
    key_i = lax.broadcasted_iota(jnp.int32, (2 * W, W), 0)
    qry_i = lax.broadcasted_iota(jnp.int32, (2 * W, W), 1)
    rel = qry_i + W - key_i
    ok = (rel >= 0) & (rel < W) & ((key_i >= W) | (i > 0))
    bias = jnp.where(ok, 0.0, MASKED)
    rep = SWA_HEADS // SWA_KV_HEADS
    q_t = (q_ref[...].astype(F32) * (Dh ** -0.5 * LOG2E)).T.astype(BF16)
    v_bands = []
    for g in range(SWA_KV_HEADS):
        gs = slice(g * Dh, (g + 1) * Dh)
        k_band = jnp.concatenate([kp_ref[:, gs], kc_ref[:, gs]], axis=0)
        v_bands.append(jnp.concatenate([vp_ref[:, gs], vc_ref[:, gs]], axis=0))
        for r in range(rep):
            hq = g * rep + r
            sc_ref[hq] = _dot(k_band, q_t[hq * Dh:(hq + 1) * Dh, :])
    for hq in range(SWA_HEADS):
        hs = slice(hq * Dh, (hq + 1) * Dh)
        sc = sc_ref[hq] + bias
        sink = sink_ref[hq] * LOG2E
        m = jnp.maximum(jnp.max(sc, axis=0, keepdims=True), sink)
        p = jnp.exp2(sc - m)
        den = jnp.sum(p, axis=0, keepdims=True) + jnp.exp2(sink - m)
        out_t_ref[hs, :] = _dot_tn(v_bands[hq // rep], p.astype(BF16)) / den
    o_ref[...] = _rms_rows(out_t_ref[...].T, ng_ref[...]).astype(BF16)


def swa_attention(sqkv, sinks, norm_g):
    b, s, _ = sqkv.shape
    W = SWA_BLOCK
    prev = lambda bi, i: jnp.maximum(i - 1, 0)
    return pl.pallas_call(
        _swa_kernel,
        grid=(b, s // W),
        in_specs=[
            pl.BlockSpec(memory_space=pltpu.SMEM),
            pl.BlockSpec((None, W, SWA_WIDTH), lambda bi, i: (bi, i, 0)),
            pl.BlockSpec((None, W, SWA_KV_WIDTH), lambda bi, i: (bi, prev(bi, i), 2)),
            pl.BlockSpec((None, W, SWA_KV_WIDTH), lambda bi, i: (bi, i, 2)),
            pl.BlockSpec((None, W, SWA_KV_WIDTH), lambda bi, i: (bi, prev(bi, i), 3)),
            pl.BlockSpec((None, W, SWA_KV_WIDTH), lambda bi, i: (bi, i, 3)),
            pl.BlockSpec((1, SWA_WIDTH), lambda bi, i: (0, 0)),
        ],
        out_specs=pl.BlockSpec((None, W, SWA_WIDTH), lambda bi, i: (bi, i, 0)),
        out_shape=jax.ShapeDtypeStruct((b, s, SWA_WIDTH), BF16),
        scratch_shapes=[pltpu.VMEM((SWA_HEADS, 2 * W, W), F32), pltpu.VMEM((SWA_WIDTH, W), F32)],
        compiler_params=_cparams(("arbitrary", "arbitrary")),
        name="swa_attention",
    )(sinks, sqkv, sqkv, sqkv, sqkv, sqkv, norm_g)


def _outproj_kernel(x_ref, ys_ref, ym_ref, yw_ref, w_ref, mod_ref, g_ref, xo_ref, h_ref, ht_ref):
    ycat = jnp.concatenate([ys_ref[...], ym_ref[...], yw_ref[...]], axis=-1)
    xn = x_ref[...] + mod_ref[2:3, :] * _dot(ycat, w_ref[...])
    xo_ref[...] = xn
    h = _rms_rows(xn, g_ref[...]) * (1.0 + mod_ref[4:5, :]) + mod_ref[3:4, :]
    h_ref[...] = h.astype(BF16)
    ht_ref[...] = h.T.astype(BF16)


def out_proj(x, y_ssd, y_moba, y_swa, w_out, mod, g, tm=512):
    b, s, d = x.shape
    nt = s // tm
    tok = lambda n: pl.BlockSpec((None, tm, n), lambda bi, i: (bi, i, 0))
    return pl.pallas_call(
        _outproj_kernel,
        grid=(b, nt),
        in_specs=[tok(d), tok(SSD_WIDTH), tok(MOBA_WIDTH), tok(SWA_WIDTH),
                  pl.BlockSpec((d, d), lambda bi, i: (0, 0)),
                  pl.BlockSpec((None, 6, d), lambda bi, i: (bi, 0, 0)),
                  pl.BlockSpec((1, d), lambda bi, i: (0, 0))],
        out_specs=[tok(d), tok(d), pl.BlockSpec((d, tm), lambda bi, i: (0, bi * nt + i))],
        out_shape=[jax.ShapeDtypeStruct((b, s, d), F32), jax.ShapeDtypeStruct((b, s, d), BF16),
                   jax.ShapeDtypeStruct((d, b * s), BF16)],
        compiler_params=_cparams(("arbitrary", "arbitrary")),
        name="out_proj",
    )(x, y_ssd, y_moba, y_swa, w_out, mod, g)


_CAND = [(r, c) for r in range(PEER_TOPK) for c in range(PEER_TOPK) if (r + 1) * (c + 1) <= PEER_TOPK]
_CAND_ROWS = -(-len(_CAND) // 8) * 8
_NL_SPLIT = 4
_NL_TOP = PEER_TOPK // (_NL_SPLIT + 1)


def _top_rows(s_t, k, with_rank=False):
    rows = []
    cur = s_t
    rank = jnp.full(s_t.shape, float(k), F32)
    for r in range(k):
        m = jnp.max(cur, axis=0, keepdims=True)
        rows.append(m)
        hit = cur == m
        cur = jnp.where(hit, -jnp.inf, cur)
        if with_rank:
            rank = jnp.where(hit, float(r), rank)
    return (rows, rank) if with_rank else rows


def _route_kernel(h_ref, wq_ref, k1_ref, k2_ref, nl_ref, c1_ref, r2_ref, e2_ref, cand_ref):
    tm = h_ref.shape[0]
    q = _dot(h_ref[...], wq_ref[...])
    k1 = k1_ref[...]
    k2 = k2_ref[...]
    cand_ref[len(_CAND):, :] = jnp.full((_CAND_ROWS - len(_CAND), tm), -jnp.inf, F32)
    for h in range(PEER_HEADS):
        q1 = q[:, (2 * h) * PEER_HALF:(2 * h + 1) * PEER_HALF].astype(BF16)
        q2 = q[:, (2 * h + 1) * PEER_HALF:(2 * h + 2) * PEER_HALF].astype(BF16)
        s1 = _dot_nt(k1, q1)
        s2 = _dot_nt(k2, q2)
        v1 = _top_rows(s1, PEER_TOPK)
        v2, rank2 = _top_rows(s2, PEER_TOPK, with_rank=True)
        for idx, (r, c) in enumerate(_CAND):
            cand_ref[idx:idx + 1, :] = v1[r] + v2[c]
        cand = cand_ref[...]
        cur = cand
        for _ in range(PEER_TOPK - 1):
            cur = jnp.where(cur == jnp.max(cur, axis=0, keepdims=True), -jnp.inf, cur)
        theta = jnp.max(cur, axis=0, keepdims=True)
        top = v1[0] + v2[0]
        zsum = jnp.sum(jnp.where(cand >= theta, jnp.exp(cand - top), 0.0), axis=0, keepdims=True)
        nlim = jnp.zeros(s1.shape, F32)
        for c in range(_NL_SPLIT):
            nlim = nlim + jnp.where(s1 + v2[c] >= theta, 1.0, 0.0)
        for r in range(_NL_TOP):
            tail = jnp.zeros_like(theta)
            for c in range(_NL_SPLIT, PEER_TOPK):
                if (r + 1) * (c + 1) <= PEER_TOPK:
                    tail = tail + jnp.where(v1[r] + v2[c] >= theta, 1.0, 0.0)
            nlim = nlim + jnp.where(s1 == v1[r], tail, 0.0)
        nl_ref[h] = nlim
        c1_ref[h] = jnp.exp(s1 - v1[0]) * (0.5 / zsum)
        r2_ref[h] = rank2.astype(BF16)
        e2_ref[h] = jnp.exp(s2 - v2[0]).astype(BF16)


def peer_route(h2, wq, k1, k2, tm=256):
    t, d = h2.shape
    big = lambda: pl.BlockSpec((PEER_HEADS, PEER_NKEYS, tm), lambda i: (0, 0, i))
    bigs = lambda dt: jax.ShapeDtypeStruct((PEER_HEADS, PEER_NKEYS, t), dt)
    return pl.pallas_call(
        _route_kernel,
        grid=(t // tm,),
        in_specs=[
            pl.BlockSpec((tm, d), lambda i: (i, 0)),
            pl.BlockSpec((d, 2 * PEER_HEADS * PEER_HALF), lambda i: (0, 0)),
            pl.BlockSpec((PEER_NKEYS, PEER_HALF), lambda i: (0, 0)),
            pl.BlockSpec((PEER_NKEYS, PEER_HALF), lambda i: (0, 0)),
        ],
        out_specs=[big(), big(), big(), big()],
        out_shape=[bigs(F32), bigs(F32), bigs(BF16), bigs(BF16)],
        scratch_shapes=[pltpu.VMEM((_CAND_ROWS, tm), F32)],
        compiler_params=_cparams(("arbitrary",)),
        name="peer_route",
    )(h2, wq, k1, k2)


_SLAB = 16
_LANES = 256
PEER_TM = 512
PEER_TE = 2048
PEER_SUB = 512
_PRE_AHEAD = 2


def _peer_kernel(ht_ref, u_ref, vt_ref, nl_ref, c1_ref, r2_ref, e2_ref, x_ref, mod_ref,
                 o_ref, pre_ref, act_ref, acc_ref):
    j = pl.program_id(1)
    tm = ht_ref.shape[1]
    te = u_ref.shape[0]
    groups = PEER_SUB // PEER_NKEYS

    @pl.when(j == 0)
    def _():
        acc_ref[...] = jnp.zeros_like(acc_ref)

    nsub = te // PEER_SUB
    sub_rows = [slice(sc * PEER_SUB, (sc + 1) * PEER_SUB) for sc in range(nsub)]
    def acc_dot(sc):
        acc_ref[...] = acc_ref[...] + _dot(vt_ref[:, sub_rows[sc]], act_ref[sc])

    def pre_dot(sc):
        pre_ref[sc] = _dot(u_ref[sub_rows[sc], :], ht_ref[...])

    for sc in range(min(_PRE_AHEAD, nsub)):
        pre_dot(sc)
    for sc in range(nsub):
        if sc + _PRE_AHEAD < nsub:
            pre_dot(sc + _PRE_AHEAD)
        if sc >= 1:
            acc_dot(sc - 1)
        for gi in range(groups):
            local = sc * groups + gi
            base8 = pl.multiple_of(j * (te // PEER_NKEYS) + (local // 8) * 8, 8)
            r8 = local % 8
            for lt in range(tm // _LANES):
                lanes = slice(lt * _LANES, (lt + 1) * _LANES)
                nslab = PEER_NKEYS // _SLAB
                slabs = [slice(sl * _SLAB, (sl + 1) * _SLAB) for sl in range(nslab)]
                gsum = [jnp.zeros((_SLAB, _LANES), BF16)] * nslab
                for h in range(PEER_HEADS):
                    nl_row = jnp.broadcast_to(nl_ref[h, pl.ds(base8, 8), lanes][r8:r8 + 1, :],
                                              (_SLAB, _LANES)).astype(BF16)
                    c1_row = jnp.broadcast_to(c1_ref[h, pl.ds(base8, 8), lanes][r8:r8 + 1, :],
                                              (_SLAB, _LANES)).astype(BF16)
                    for sl, rows in enumerate(slabs):
                        hit = r2_ref[h, rows, lanes] < nl_row
                        gsum[sl] = gsum[sl] + jnp.where(hit, e2_ref[h, rows, lanes], jnp.zeros((), BF16)) * c1_row
                for sl in range(nslab):
                    prow = slice(gi * PEER_NKEYS + sl * _SLAB, gi * PEER_NKEYS + (sl + 1) * _SLAB)
                    pre = pre_ref[sc, prow, lanes]
                    gelu2 = pre * (1.0 + lax.erf(pre * (2.0 ** -0.5)))
                    act_ref[sc, prow, lanes] = gelu2.astype(BF16) * gsum[sl]
    acc_dot(nsub - 1)

    @pl.when(j == pl.num_programs(1) - 1)
    def _():
        o_ref[...] = x_ref[...] + mod_ref[5:6, :] * acc_ref[...].T


def peer_experts(h2t, u, vt, nl, c1, r2, e2, x, mod, seq, tm=PEER_TM, te=PEER_TE):
    d, t = h2t.shape
    ne = u.shape[0]
    big = lambda: pl.BlockSpec((PEER_HEADS, PEER_NKEYS, tm), lambda i, j: (0, 0, i))
    nsub = te // PEER_SUB
    return pl.pallas_call(
        _peer_kernel,
        grid=(t // tm, ne // te),
        in_specs=[
            pl.BlockSpec((d, tm), lambda i, j: (0, i)),
            pl.BlockSpec((te, d), lambda i, j: (j, 0)),
            pl.BlockSpec((d, te), lambda i, j: (0, j)),
            big(), big(), big(), big(),
            pl.BlockSpec((tm, d), lambda i, j: (i, 0)),
            pl.BlockSpec((None, 6, d), lambda i, j: ((i * tm) // seq, 0, 0)),
        ],
        out_specs=pl.BlockSpec((tm, d), lambda i, j: (i, 0)),
        out_shape=jax.ShapeDtypeStruct((t, d), F32),
        scratch_shapes=[pltpu.VMEM((nsub, PEER_SUB, tm), F32), pltpu.VMEM((nsub, PEER_SUB, tm), BF16),
                        pltpu.VMEM((d, tm), F32)],
        compiler_params=_cparams(("arbitrary", "arbitrary")),
        name="peer_experts",
    )(h2t, u, vt, nl, c1, r2, e2, x, mod)


def _final_kernel(x_ref, g_ref, o_ref):
    o_ref[...] = _rms_rows(x_ref[...], g_ref[...])


def final_norm(x, g, tm=1024):
    t, d = x.shape
    return pl.pallas_call(
        _final_kernel,
        grid=(t // tm,),
        in_specs=[pl.BlockSpec((tm, d), lambda i: (i, 0)), pl.BlockSpec((1, d), lambda i: (0, 0))],
        out_specs=pl.BlockSpec((tm, d), lambda i: (i, 0)),
        out_shape=jax.ShapeDtypeStruct((t, d), F32),
        compiler_params=_cparams(("arbitrary",)),
        name="final_norm",
    )(x, g)


def _reorder_w_in(w_in):
    o_dt = SSD_WIDTH + XBC_WIDTH
    o_rest = o_dt + SSD_HEADS
    pad = jnp.zeros(w_in.shape[:-1] + (DT_PAD - SSD_HEADS,), w_in.dtype)
    return jnp.concatenate([w_in[..., :o_dt], w_in[..., o_rest:], w_in[..., o_dt:o_rest], pad], axis=-1)


def _pad_heads(p):
    return jnp.pad(p, ((0, 0), (0, DT_PAD - SSD_HEADS)))


def kernel(x, c, ada_w, ada_b, norm1_g, norm2_g, w_in, conv_w, conv_b, dt_bias, a_log, d_skip, ssd_norm_g, moba_norm_g, swa_sinks, swa_norm_g, w_out, peer_wq, peer_k1, peer_k2, peer_u, peer_v, final_g):
    b, s, d = x.shape
    depth = ada_w.shape[0]
    mod_all = ada_mod(c, ada_w, ada_b).reshape(depth, b, 6, d)
    w_in_b = _reorder_w_in(w_in.astype(BF16))
    w_out_b = w_out.astype(BF16)
    wq_b = peer_wq.astype(BF16)
    k1_b = peer_k1.astype(BF16)
    k2_b = peer_k2.astype(BF16)
    dtb = _pad_heads(dt_bias)
    alog = _pad_heads(a_log)
    dskip_x = jnp.repeat(d_skip, HEAD_DIM, axis=-1)
    for l in range(depth):
        mod = mod_all[l]
        z, xbc, dt, mqkv, sqkv = in_proj(x, mod, norm1_g[l][None], w_in_b[l])
        y_ssd = ssd_mixer(z, xbc, dt, conv_w[l], conv_b[l][None], dtb[l][None], alog[l][None],
                          dskip_x[l][None], ssd_norm_g[l][None])
        y_moba = moba_attention(mqkv, moba_norm_g[l][None])
        y_swa = swa_attention(sqkv, swa_sinks[l], swa_norm_g[l][None])
        x, h2, h2t = out_proj(x, y_ssd, y_moba, y_swa, w_out_b[l], mod, norm2_g[l][None])
        nl, c1, r2, e2 = peer_route(h2.reshape(b * s, d), wq_b[l], k1_b[l], k2_b[l])
        x = peer_experts(h2t, peer_u[l].astype(BF16), peer_v[l].T.astype(BF16), nl, c1, r2, e2,
                         x.reshape(b * s, d), mod, s).reshape(b, s, d)
    return final_norm(x.reshape(b * s, d), final_g[None]).reshape(b, s, d)
```

```python
import functools

import numpy as np
import jax
import jax.numpy as jnp
from jax import lax
from jax.experimental import pallas as pl
from jax.experimental.pallas import tpu as pltpu

F32 = jnp.float32
BF16 = jnp.bfloat16
HIGHEST = lax.Precision.HIGHEST

EPS = 1e-5
D_MODEL = 1024
HEAD_DIM = 64
SSD_HEADS = 8
SSD_WIDTH = 512
SSD_GROUPS = 2
SSD_STATE = 128
SSD_CHUNK = 128
CONV_K = 4
XBC_WIDTH = 1024
MOBA_HEADS = 4
MOBA_WIDTH = 256
MOBA_BLOCK = 256
MOBA_TOPK = 3
SWA_HEADS = 4
SWA_KV_HEADS = 2
SWA_WIDTH = 256
SWA_KV_WIDTH = 128
SWA_BLOCK = 128
PEER_HEADS = 8
PEER_TOPK = 16
PEER_NKEYS = 128
PEER_EXPERTS = PEER_NKEYS * PEER_NKEYS
PEER_HALF = 128

DT_PAD = 128
N_IN_PAD = SSD_WIDTH + XBC_WIDTH + 3 * MOBA_WIDTH + SWA_WIDTH + 2 * SWA_KV_WIDTH + DT_PAD
MASKED = -1e30
LOG2E = 1.4426950408889634

VMEM_LIMIT = 56 * 1024 * 1024


def _cparams(sem):
    return pltpu.CompilerParams(dimension_semantics=sem, vmem_limit_bytes=VMEM_LIMIT)


def _dot(a, b, precision=None):
    return jnp.dot(a, b, preferred_element_type=F32, precision=precision)


def _dot_nt(a, b, precision=None):
    return lax.dot_general(a, b, (((1,), (1,)), ((), ())), preferred_element_type=F32,
                           precision=precision)


def _dot_tn(a, b, precision=None):
    return lax.dot_general(a, b, (((0,), (0,)), ((), ())), preferred_element_type=F32,
                           precision=precision)


def _silu(x):
    return x / (1.0 + jnp.exp(-x))


def _rms_rows(x, g):
    ms = jnp.mean(x * x, axis=-1, keepdims=True)
    return x * lax.rsqrt(ms + EPS) * g


def _ada_kernel(c_ref, w_ref, b_ref, o_ref):
    cond = _silu(c_ref[...])
    o_ref[...] = _dot(cond, w_ref[...], HIGHEST) + b_ref[...]


def ada_mod(c, ada_w, ada_b, tn=2048):
    nl, d, n6 = ada_w.shape
    b = c.shape[0]
    return pl.pallas_call(
        _ada_kernel,
        grid=(nl, n6 // tn),
        in_specs=[
            pl.BlockSpec((b, d), lambda l, j: (0, 0)),
            pl.BlockSpec((None, d, tn), lambda l, j: (l, 0, j)),
            pl.BlockSpec((None, 1, tn), lambda l, j: (l, 0, j)),
        ],
        out_specs=pl.BlockSpec((None, b, tn), lambda l, j: (l, 0, j)),
        out_shape=jax.ShapeDtypeStruct((nl, b, n6), F32),
        compiler_params=_cparams(("arbitrary", "arbitrary")),
        name="ada_mod",
    )(c, ada_w, ada_b.reshape(nl, 1, n6))


_O_Z = 0
_O_XBC = _O_Z + SSD_WIDTH
_O_MQKV = _O_XBC + XBC_WIDTH
_O_SQKV = _O_MQKV + 3 * MOBA_WIDTH
_O_DT = _O_SQKV + SWA_WIDTH + 2 * SWA_KV_WIDTH


def _inproj_kernel(x_ref, mod_ref, g_ref, w_ref, z_ref, xbc_ref, dt_ref, mqkv_ref, sqkv_ref):
    h = _rms_rows(x_ref[...], g_ref[...]) * (1.0 + mod_ref[1:2, :]) + mod_ref[0:1, :]
    hb = h.astype(BF16)
    z_ref[...] = _dot(hb, w_ref[:, _O_Z:_O_XBC])
    xbc_ref[...] = _dot(hb, w_ref[:, _O_XBC:_O_MQKV])
    mqkv_ref[...] = _dot(hb, w_ref[:, _O_MQKV:_O_SQKV]).astype(BF16)
    sqkv_ref[...] = _dot(hb, w_ref[:, _O_SQKV:_O_DT]).astype(BF16)
    dt_ref[...] = _dot(hb, w_ref[:, _O_DT:N_IN_PAD])


def in_proj(x, mod, g, w, tm=512):
    b, s, d = x.shape
    tok = lambda n: pl.BlockSpec((None, tm, n), lambda bi, i: (bi, i, 0))
    return pl.pallas_call(
        _inproj_kernel,
        grid=(b, s // tm),
        in_specs=[
            tok(d),
            pl.BlockSpec((None, 6, d), lambda bi, i: (bi, 0, 0)),
            pl.BlockSpec((1, d), lambda bi, i: (0, 0)),
            pl.BlockSpec((d, N_IN_PAD), lambda bi, i: (0, 0)),
        ],
        out_specs=[tok(SSD_WIDTH), tok(XBC_WIDTH), tok(DT_PAD), tok(3 * MOBA_WIDTH),
                   tok(SWA_WIDTH + 2 * SWA_KV_WIDTH)],
        out_shape=[
            jax.ShapeDtypeStruct((b, s, SSD_WIDTH), F32),
            jax.ShapeDtypeStruct((b, s, XBC_WIDTH), F32),
            jax.ShapeDtypeStruct((b, s, DT_PAD), F32),
            jax.ShapeDtypeStruct((b, s, 3 * MOBA_WIDTH), BF16),
            jax.ShapeDtypeStruct((b, s, SWA_WIDTH + 2 * SWA_KV_WIDTH), BF16),
        ],
        compiler_params=_cparams(("arbitrary", "arbitrary")),
        name="in_proj",
    )(x, mod, g, w)


def _ssd_kernel(z_ref, xbc_ref, dt_ref, cw_ref, cb_ref, dtb_ref, alog_ref, dskip_ref, ng_ref,
                o_ref, xpad_ref, state_ref):
    L, N, P = SSD_CHUNK, SSD_STATE, HEAD_DIM
    GW = SSD_WIDTH // SSD_GROUPS
    c = pl.program_id(1)

    @pl.when(c == 0)
    def _():
        xpad_ref[0:8, :] = jnp.zeros((8, XBC_WIDTH), F32)
        state_ref[...] = jnp.zeros_like(state_ref)

    xpad_ref[8:8 + L, :] = xbc_ref[...]
    y = cb_ref[...] + cw_ref[0:1, :] * xpad_ref[5:5 + L, :]
    for k in range(1, CONV_K):
        y = y + cw_ref[k:k + 1, :] * xpad_ref[5 + k:5 + k + L, :]
    xpad_ref[0:8, :] = xpad_ref[L:L + 8, :]
    xc = _silu(y)
    xs = xc[:, :SSD_WIDTH]

    dtr = dt_ref[...] + dtb_ref[...]
    dtv = jnp.maximum(dtr, 0.0) + jnp.log1p(jnp.exp(-jnp.abs(dtr)))
    a = dtv * (-jnp.exp(alog_ref[...]))
    row = lax.broadcasted_iota(jnp.int32, (L, L), 0)
    col = lax.broadcasted_iota(jnp.int32, (L, L), 1)
    causal = col <= row
    a_cs = _dot(causal.astype(F32), a, HIGHEST)
    a_cs_t = a_cs.T
    er = lax.broadcasted_iota(jnp.int32, (DT_PAD, SSD_WIDTH), 0)
    ec = lax.broadcasted_iota(jnp.int32, (DT_PAD, SSD_WIDTH), 1)
    expand = (ec // P == er).astype(F32)
    dt_x = _dot(dtv, expand, HIGHEST)
    acs_x = _dot(a_cs, expand, HIGHEST)
    atot_x = acs_x[L - 1:L, :]

    xdt = xs * dt_x
    w_state = (xdt * jnp.exp(atot_x - acs_x)).astype(BF16)
    decay_in = jnp.exp(acs_x)
    decay_tot = jnp.exp(atot_x)
    lane_head = lax.broadcasted_iota(jnp.int32, (L, GW), 1) // P

    ys = []
    for g in range(SSD_GROUPS):
        bm = xc[:, SSD_WIDTH + g * N:SSD_WIDTH + (g + 1) * N].astype(BF16)
        cm = xc[:, SSD_WIDTH + SSD_GROUPS * N + g * N:SSD_WIDTH + SSD_GROUPS * N + (g + 1) * N].astype(BF16)
        cbm = _dot_nt(cm, bm)
        xdt_g = xdt[:, g * GW:(g + 1) * GW]
        y_g = jnp.zeros((L, GW), F32)
        for r in range(SSD_HEADS // SSD_GROUPS):
            hh = g * (SSD_HEADS // SSD_GROUPS) + r
            seg = jnp.broadcast_to(a_cs[:, hh:hh + 1], (L, L)) - jnp.broadcast_to(a_cs_t[hh:hh + 1, :], (L, L))
            decay = jnp.where(causal, jnp.exp(jnp.where(causal, seg, 0.0)), 0.0)
            m_h = (cbm * decay).astype(BF16)
            x_h = jnp.where(lane_head == r, xdt_g, 0.0).astype(BF16)
            y_g = y_g + _dot(m_h, x_h)
        st = state_ref[g]
        y_g = y_g + decay_in[:, g * GW:(g + 1) * GW] * _dot(cm, st.astype(BF16))
        state_ref[g] = decay_tot[:, g * GW:(g + 1) * GW] * st + _dot_tn(bm, w_state[:, g * GW:(g + 1) * GW])
        ys.append(y_g)
    yfull = jnp.concatenate(ys, axis=-1) + dskip_ref[...] * xs
    gated = yfull * _silu(z_ref[...])
    o_ref[...] = _rms_rows(gated, ng_ref[...]).astype(BF16)


def ssd_mixer(z, xbc, dt, conv_w, conv_b, dt_bias, a_log, d_skip_x, norm_g):
    b, s, _ = z.shape
    L = SSD_CHUNK
    tok = lambda n: pl.BlockSpec((None, L, n), lambda bi, i: (bi, i, 0))
    par = lambda r, n: pl.BlockSpec((r, n), lambda bi, i: (0, 0))
    return pl.pallas_call(
        _ssd_kernel,
        grid=(b, s // L),
        in_specs=[tok(SSD_WIDTH), tok(XBC_WIDTH), tok(DT_PAD), par(CONV_K, XBC_WIDTH), par(1, XBC_WIDTH),
                  par(1, DT_PAD), par(1, DT_PAD), par(1, SSD_WIDTH), par(1, SSD_WIDTH)],
        out_specs=tok(SSD_WIDTH),
        out_shape=jax.ShapeDtypeStruct((b, s, SSD_WIDTH), BF16),
        scratch_shapes=[pltpu.VMEM((L + 8, XBC_WIDTH), F32),
                        pltpu.VMEM((SSD_GROUPS, SSD_STATE, SSD_WIDTH // SSD_GROUPS), F32)],
        compiler_params=_cparams(("arbitrary", "arbitrary")),
        name="ssd_mixer",
    )(z, xbc, dt, conv_w, conv_b, dt_bias, a_log, d_skip_x, norm_g)


def _moba_kernel(q_ref, k_ref, v_ref, ng_ref, o_ref, kmean_ref, vt_ref, qt_ref, selb_ref, sc_ref, p_ref,
                 acc_ref):
    BL, Dh = MOBA_BLOCK, HEAD_DIM
    s = k_ref.shape[0]
    nb = s // BL
    i = pl.program_id(1)

    @pl.when(i == 0)
    def _():
        blk = lax.broadcasted_iota(jnp.int32, (nb, s), 0)
        pos = lax.broadcasted_iota(jnp.int32, (nb, s), 1)
        avg = jnp.where(pos // BL == blk, 1.0 / BL, 0.0).astype(BF16)
        kmean_ref[...] = _dot(avg, k_ref[...])
        for n in range(nb):
            vt_ref[:, n * BL:(n + 1) * BL] = v_ref[n * BL:(n + 1) * BL, :].astype(F32).T.astype(BF16)

    key_i = lax.broadcasted_iota(jnp.int32, (BL, BL), 0)
    qry_i = lax.broadcasted_iota(jnp.int32, (BL, BL), 1)
    own_bias = jnp.where(key_i <= qry_i, 0.0, MASKED)
    past = lax.broadcasted_iota(jnp.int32, (nb, BL), 0) < i
    heads = [slice(h * Dh, (h + 1) * Dh) for h in range(MOBA_HEADS)]

    q_t = (q_ref[...].astype(F32) * (Dh ** -0.5 * LOG2E)).T
    qt_ref[...] = q_t.astype(BF16)
    for h, hs in enumerate(heads):
        gate = _dot(kmean_ref[:, hs], q_t[hs, :], HIGHEST)
        cur = jnp.where(past, gate, -jnp.inf)
        for _ in range(MOBA_TOPK - 1):
            cur = jnp.where(cur == jnp.max(cur, axis=0, keepdims=True), -jnp.inf, cur)
        thr = jnp.max(cur, axis=0, keepdims=True)
        selb_ref[h] = jnp.where(past & (gate >= thr), 0.0, MASKED)

    def chunk(start, nblk, biases, ms, ls, first):
        rows = pl.ds(pl.multiple_of(start, BL), nblk * BL)
        for h, hs in enumerate(heads):
            sc_ref[h, :nblk * BL, :] = _dot(k_ref[rows, hs], qt_ref[hs, :])
        ms_new, ls_new, pending = [], [], []
        for h, hs in enumerate(heads):
            scs = []
            m = ms[h]
            for t in range(nblk):
                sc = sc_ref[h, t * BL:(t + 1) * BL, :]
                if biases[h][t].shape[0] == 1:
                    m = jnp.maximum(m, jnp.max(sc, axis=0, keepdims=True) + biases[h][t])
                else:
                    sc = sc + biases[h][t]
                    m = jnp.maximum(m, jnp.max(sc, axis=0, keepdims=True))
                scs.append(sc)
            psum = jnp.zeros((1, BL), F32)
            for t, sc in enumerate(scs):
                shift = m - biases[h][t] if biases[h][t].shape[0] == 1 else m
                p = jnp.exp2(sc - shift)
                psum = psum + jnp.sum(p, axis=0, keepdims=True)
                p_ref[h, t * BL:(t + 1) * BL, :] = p.astype(BF16)
            pv = _dot(vt_ref[hs, rows], p_ref[h, :nblk * BL, :])
            alpha = None if first else jnp.exp2(ms[h] - m)
            ls_new.append(psum if first else alpha * ls[h] + psum)
            ms_new.append(m)
            for hsp, pvp, alphap in pending:
                acc_ref[hsp, :] = pvp if first else alphap * acc_ref[hsp, :] + pvp
            pending = [(hs, pv, alpha)]
        for hsp, pvp, alphap in pending:
            acc_ref[hsp, :] = pvp if first else alphap * acc_ref[hsp, :] + pvp
        return tuple(ms_new), tuple(ls_new)

    init = tuple(jnp.full((1, BL), MASKED, F32) for _ in heads)

    def first_single():
        return chunk(i * BL, 1, [[own_bias]] * MOBA_HEADS, init, None, True)

    def first_pair():
        biases = [[selb_ref[h, pl.ds(i - 1, 1), :], own_bias] for h in range(MOBA_HEADS)]
        return chunk((i - 1) * BL, 2, biases, init, None, True)

    carry = lax.cond(i % 2 == 1, first_pair, first_single)

    def body(n, carry):
        biases = [[selb_ref[h, pl.ds(2 * n, 1), :], selb_ref[h, pl.ds(2 * n + 1, 1), :]]
                  for h in range(MOBA_HEADS)]
        return chunk(2 * n * BL, 2, biases, carry[0], carry[1], False)

    ms, ls = lax.fori_loop(0, i // 2, body, carry)
    for h, hs in enumerate(heads):
        acc_ref[hs, :] = acc_ref[hs, :] / ls[h]
    o_ref[...] = _rms_rows(acc_ref[...].T, ng_ref[...]).astype(BF16)


def moba_attention(mqkv, norm_g):
    b, s, _ = mqkv.shape
    BL = MOBA_BLOCK
    nb = s // BL
    return pl.pallas_call(
        _moba_kernel,
        grid=(b, nb),
        in_specs=[
            pl.BlockSpec((None, BL, MOBA_WIDTH), lambda bi, i: (bi, i, 0)),
            pl.BlockSpec((None, s, MOBA_WIDTH), lambda bi, i: (bi, 0, 1)),
            pl.BlockSpec((None, s, MOBA_WIDTH), lambda bi, i: (bi, 0, 2)),
            pl.BlockSpec((1, MOBA_WIDTH), lambda bi, i: (0, 0)),
        ],
        out_specs=pl.BlockSpec((None, BL, MOBA_WIDTH), lambda bi, i: (bi, i, 0)),
        out_shape=jax.ShapeDtypeStruct((b, s, MOBA_WIDTH), BF16),
        scratch_shapes=[pltpu.VMEM((nb, MOBA_WIDTH), F32), pltpu.VMEM((MOBA_WIDTH, s), BF16),
                        pltpu.VMEM((MOBA_WIDTH, BL), BF16), pltpu.VMEM((MOBA_HEADS, nb, BL), F32),
                        pltpu.VMEM((MOBA_HEADS, 2 * BL, BL), F32), pltpu.VMEM((MOBA_HEADS, 2 * BL, BL), BF16),
                        pltpu.VMEM((MOBA_WIDTH, BL), F32)],
        compiler_params=_cparams(("arbitrary", "arbitrary")),
        name="moba_attention",
    )(mqkv, mqkv, mqkv, norm_g)


def _swa_kernel(sink_ref, q_ref, kp_ref, kc_ref, vp_ref, vc_ref, ng_ref, o_ref, sc_ref, out_t_ref):
    W, Dh = SWA_BLOCK, HEAD_DIM
    i = pl.program_id(1)
    key_i = lax.broadcasted_iota(jnp.int32, (2 * W, W), 0)
    qry_i = lax.broadcasted_iota(jnp.int32, (2 * W, W), 1)
    rel = qry_i + W - key_i
    ok = (rel >= 0) & (rel < W) & ((key_i >= W) | (i > 0))
    bias = jnp.where(ok, 0.0, MASKED)
    rep = SWA_HEADS // SWA_KV_HEADS
    q_t = (q_ref[...].astype(F32) * (Dh ** -0.5 * LOG2E)).T.astype(BF16)
    v_bands = []
    for g in range(SWA_KV_HEADS):
        gs = slice(g * Dh, (g + 1) * Dh)
        k_band = jnp.concatenate([kp_ref[:, gs], kc_ref[:, gs]], axis=0)
        v_bands.append(jnp.concatenate([vp_ref[:, gs], vc_ref[:, gs]], axis=0))
        for r in range(rep):
            hq = g * rep + r
            sc_ref[hq] = _dot(k_band, q_t[hq * Dh:(hq + 1) * Dh, :])
    for hq in range(SWA_HEADS):
        hs = slice(hq * Dh, (hq + 1) * Dh)
        sc = sc_ref[hq] + bias
        sink = sink_ref[hq] * LOG2E
        m = jnp.maximum(jnp.max(sc, axis=0, keepdims=True), sink)
        p = jnp.exp2(sc - m)
        den = jnp.sum(p, axis=0, keepdims=True) + jnp.exp2(sink - m)
        out_t_ref[hs, :] = _dot_tn(v_bands[hq // rep], p.astype(BF16)) / den
    o_ref[...] = _rms_rows(out_t_ref[...].T, ng_ref[...]).astype(BF16)


def swa_attention(sqkv, sinks, norm_g):
    b, s, _ = sqkv.shape
    W = SWA_BLOCK
    prev = lambda bi, i: jnp.maximum(i - 1, 0)
    return pl.pallas_call(
        _swa_kernel,
        grid=(b, s // W),
        in_specs=[
            pl.BlockSpec(memory_space=pltpu.SMEM),
            pl.BlockSpec((None, W, SWA_WIDTH), lambda bi, i: (bi, i, 0)),
            pl.BlockSpec((None, W, SWA_KV_WIDTH), lambda bi, i: (bi, prev(bi, i), 2)),
            pl.BlockSpec((None, W, SWA_KV_WIDTH), lambda bi, i: (bi, i, 2)),
            pl.BlockSpec((None, W, SWA_KV_WIDTH), lambda bi, i: (bi, prev(bi, i), 3)),
            pl.BlockSpec((None, W, SWA_KV_WIDTH), lambda bi, i: (bi, i, 3)),
            pl.BlockSpec((1, SWA_WIDTH), lambda bi, i: (0, 0)),
        ],
        out_specs=pl.BlockSpec((None, W, SWA_WIDTH), lambda bi, i: (bi, i, 0)),
        out_shape=jax.ShapeDtypeStruct((b, s, SWA_WIDTH), BF16),
        scratch_shapes=[pltpu.VMEM((SWA_HEADS, 2 * W, W), F32), pltpu.VMEM((SWA_WIDTH, W), F32)],
        compiler_params=_cparams(("arbitrary", "arbitrary")),
        name="swa_attention",
    )(sinks, sqkv, sqkv, sqkv, sqkv, sqkv, norm_g)


def _outproj_kernel(x_ref, ys_ref, ym_ref, yw_ref, w_ref, mod_ref, g_ref, xo_ref, h_ref, ht_ref):
    ycat = jnp.concatenate([ys_ref[...], ym_ref[...], yw_ref[...]], axis=-1)
    xn = x_ref[...] + mod_ref[2:3, :] * _dot(ycat, w_ref[...])
    xo_ref[...] = xn
    h = _rms_rows(xn, g_ref[...]) * (1.0 + mod_ref[4:5, :]) + mod_ref[3:4, :]
    h_ref[...] = h.astype(BF16)
    ht_ref[...] = h.T.astype(BF16)


def out_proj(x, y_ssd, y_moba, y_swa, w_out, mod, g, tm=512):
    b, s, d = x.shape
    nt = s // tm
    tok = lambda n: pl.BlockSpec((None, tm, n), lambda bi, i: (bi, i, 0))
    return pl.pallas_call(
        _outproj_kernel,
        grid=(b, nt),
        in_specs=[tok(d), tok(SSD_WIDTH), tok(MOBA_WIDTH), tok(SWA_WIDTH),
                  pl.BlockSpec((d, d), lambda bi, i: (0, 0)),
                  pl.BlockSpec((None, 6, d), lambda bi, i: (bi, 0, 0)),
                  pl.BlockSpec((1, d), lambda bi, i: (0, 0))],
        out_specs=[tok(d), tok(d), pl.BlockSpec((d, tm), lambda bi, i: (0, bi * nt + i))],
        out_shape=[jax.ShapeDtypeStruct((b, s, d), F32), jax.ShapeDtypeStruct((b, s, d), BF16),
                   jax.ShapeDtypeStruct((d, b * s), BF16)],
        compiler_params=_cparams(("arbitrary", "arbitrary")),
        name="out_proj",
    )(x, y_ssd, y_moba, y_swa, w_out, mod, g)


_CAND = [(r, c) for r in range(PEER_TOPK) for c in range(PEER_TOPK) if (r + 1) * (c + 1) <= PEER_TOPK]
_CAND_ROWS = -(-len(_CAND) // 8) * 8
_NL_SPLIT = 4
_NL_TOP = PEER_TOPK // (_NL_SPLIT + 1)


def _oddeven_merge_sort_pairs(n):
    pairs = []
    p = 1
    while p < n:
        k = p
        while k >= 1:
            for j in range(k % p, n - k, 2 * k):
                for i in range(min(k, n - j - k)):
                    if (i + j) // (2 * p) == (i + j + k) // (2 * p):
                        pairs.append((i + j, i + j + k))
            k //= 2
        p *= 2
    return pairs


_SUBLANES = 8
_SORT16 = _oddeven_merge_sort_pairs(PEER_TOPK)


def _exchange(vals, x, y):
    vals[x], vals[y] = jnp.maximum(vals[x], vals[y]), jnp.minimum(vals[x], vals[y])


def _tiles(s_t):
    r = s_t.reshape(s_t.shape[0] // _SUBLANES, _SUBLANES, s_t.shape[1])
    return [r[k] for k in range(r.shape[0])]


def _top16_sorted(s_t):
    vals = _tiles(s_t)
    for x, y in _SORT16:
        _exchange(vals, x, y)
    shift = _SUBLANES // 2
    while shift >= 1:
        other = [pltpu.roll(v, shift, 0) for v in vals]
        vals = [jnp.maximum(vals[k], other[PEER_TOPK - 1 - k]) for k in range(PEER_TOPK)]
        d = PEER_TOPK // 2
        while d >= 1:
            for k in range(PEER_TOPK):
                if k & d == 0:
                    _exchange(vals, k, k + d)
            d //= 2
        shift //= 2
    return vals


def _rank_in(x, top):
    def pick(bits, lo, hi):
        if not bits:
            return top[(lo + hi) // 2]
        mid = (lo + hi) // 2
        return jnp.where(bits[0], pick(bits[1:], mid + 1, hi), pick(bits[1:], lo, mid - 1))

    bits = []
    rank = jnp.zeros(x.shape, F32)
    step = PEER_TOPK // 2
    while step >= 1:
        below = x < pick(bits, 0, PEER_TOPK - 2)
        rank = rank + jnp.where(below, float(step), 0.0)
        bits.append(below)
        step //= 2
    return jnp.where(x < top[PEER_TOPK - 1], float(PEER_TOPK), rank)


_ROUTE_LANES = 128


def _route_kernel(h_ref, wq_ref, k1_ref, k2_ref, nl_ref, c1_ref, r2_ref, e2_ref, cand_ref):
    tm = h_ref.shape[0]
    q = _dot(h_ref[...], wq_ref[...])
    k1 = k1_ref[...]
    k2 = k2_ref[...]
    cand_ref[len(_CAND):, :] = jnp.full((_CAND_ROWS - len(_CAND), _ROUTE_LANES), -jnp.inf, F32)
    for h in range(PEER_HEADS):
        q1 = q[:, (2 * h) * PEER_HALF:(2 * h + 1) * PEER_HALF].astype(BF16)
        q2 = q[:, (2 * h + 1) * PEER_HALF:(2 * h + 2) * PEER_HALF].astype(BF16)
        s1_all = _dot_nt(k1, q1)
        s2_all = _dot_nt(k2, q2)
        for lt in range(tm // _ROUTE_LANES):
            lanes = slice(lt * _ROUTE_LANES, (lt + 1) * _ROUTE_LANES)
            s1 = s1_all[:, lanes]
            s2 = s2_all[:, lanes]
            t1 = _top16_sorted(s1)
            t2 = _top16_sorted(s2)
            v1 = [t[0:1, :] for t in t1]
            v2 = [t[0:1, :] for t in t2]
            rank2 = jnp.concatenate([_rank_in(x, t2) for x in _tiles(s2)], axis=0)
            for idx, (r, c) in enumerate(_CAND):
                cand_ref[idx:idx + 1, :] = v1[r] + v2[c]
            cand = cand_ref[...]
            cur = cand
            for _ in range(PEER_TOPK - 1):
                cur = jnp.where(cur == jnp.max(cur, axis=0, keepdims=True), -jnp.inf, cur)
            theta = jnp.max(cur, axis=0, keepdims=True)
            top = v1[0] + v2[0]
            zsum = jnp.sum(jnp.where(cand >= theta, jnp.exp(cand - top), 0.0), axis=0, keepdims=True)
            nlim = jnp.zeros(s1.shape, F32)
            for c in range(_NL_SPLIT):
                nlim = nlim + jnp.where(s1 + v2[c] >= theta, 1.0, 0.0)
            extra = jnp.zeros(s1.shape, F32)
            for r in reversed(range(_NL_TOP)):
                tail = jnp.zeros_like(theta)
                for c in range(_NL_SPLIT, PEER_TOPK):
                    if (r + 1) * (c + 1) <= PEER_TOPK:
                        tail = tail + jnp.where(v1[r] + v2[c] >= theta, 1.0, 0.0)
                extra = jnp.where(s1 == v1[r], tail, extra)
            nl_ref[h, :, lanes] = nlim + extra
            c1_ref[h, :, lanes] = jnp.exp(s1 - v1[0]) * (0.5 / zsum)
            r2_ref[h, :, lanes] = rank2.astype(BF16)
            e2_ref[h, :, lanes] = jnp.exp(s2 - v2[0]).astype(BF16)


def peer_route(h2, wq, k1, k2, tm=256):
    t, d = h2.shape
    big = lambda: pl.BlockSpec((PEER_HEADS, PEER_NKEYS, tm), lambda i: (0, 0, i))
    bigs = lambda dt: jax.ShapeDtypeStruct((PEER_HEADS, PEER_NKEYS, t), dt)
    return pl.pallas_call(
        _route_kernel,
        grid=(t // tm,),
        in_specs=[
            pl.BlockSpec((tm, d), lambda i: (i, 0)),
            pl.BlockSpec((d, 2 * PEER_HEADS * PEER_HALF), lambda i: (0, 0)),
            pl.BlockSpec((PEER_NKEYS, PEER_HALF), lambda i: (0, 0)),
            pl.BlockSpec((PEER_NKEYS, PEER_HALF), lambda i: (0, 0)),
        ],
        out_specs=[big(), big(), big(), big()],
        out_shape=[bigs(F32), bigs(F32), bigs(BF16), bigs(BF16)],
        scratch_shapes=[pltpu.VMEM((_CAND_ROWS, _ROUTE_LANES), F32)],
        compiler_params=_cparams(("arbitrary",)),
        name="peer_route",
    )(h2, wq, k1, k2)


_SLAB = 16
_LANES = 256
PEER_TM = 512
PEER_TE = 2048
PEER_SUB = 512
_PRE_AHEAD = 2


def _peer_kernel(ht_ref, u_ref, vt_ref, nl_ref, c1_ref, r2_ref, e2_ref, x_ref, mod_ref,
                 o_ref, pre_ref, act_ref, acc_ref):
    j = pl.program_id(1)
    tm = ht_ref.shape[1]
    te = u_ref.shape[0]
    groups = PEER_SUB // PEER_NKEYS

    @pl.when(j == 0)
    def _():
        acc_ref[...] = jnp.zeros_like(acc_ref)

    nsub = te // PEER_SUB
    sub_rows = [slice(sc * PEER_SUB, (sc + 1) * PEER_SUB) for sc in range(nsub)]
    def acc_dot(sc):
        acc_ref[...] = acc_ref[...] + _dot(vt_ref[:, sub_rows[sc]], act_ref[sc])

    def pre_dot(sc):
        pre_ref[sc] = _dot(u_ref[sub_rows[sc], :], ht_ref[...])

    for sc in range(min(_PRE_AHEAD, nsub)):
        pre_dot(sc)
    for sc in range(nsub):
        if sc + _PRE_AHEAD < nsub:
            pre_dot(sc + _PRE_AHEAD)
        if sc >= 1:
            acc_dot(sc - 1)
        for gi in range(groups):
            local = sc * groups + gi
            base8 = pl.multiple_of(j * (te // PEER_NKEYS) + (local // 8) * 8, 8)
            r8 = local % 8
            for lt in range(tm // _LANES):
                lanes = slice(lt * _LANES, (lt + 1) * _LANES)
                nslab = PEER_NKEYS // _SLAB
                slabs = [slice(sl * _SLAB, (sl + 1) * _SLAB) for sl in range(nslab)]
                gsum = [jnp.zeros((_SLAB, _LANES), BF16)] * nslab
                for h in range(PEER_HEADS):
                    nl_row = jnp.broadcast_to(nl_ref[h, pl.ds(base8, 8), lanes][r8:r8 + 1, :],
                                              (_SLAB, _LANES)).astype(BF16)
                    c1_row = jnp.broadcast_to(c1_ref[h, pl.ds(base8, 8), lanes][r8:r8 + 1, :],
                                              (_SLAB, _LANES)).astype(BF16)
                    for sl, rows in enumerate(slabs):
                        hit = r2_ref[h, rows, lanes] < nl_row
                        gsum[sl] = gsum[sl] + jnp.where(hit, e2_ref[h, rows, lanes], jnp.zeros((), BF16)) * c1_row
                for sl in range(nslab):
                    prow = slice(gi * PEER_NKEYS + sl * _SLAB, gi * PEER_NKEYS + (sl + 1) * _SLAB)
                    pre = pre_ref[sc, prow, lanes]
                    gelu2 = pre * (1.0 + lax.erf(pre * (2.0 ** -0.5)))
                    act_ref[sc, prow, lanes] = gelu2.astype(BF16) * gsum[sl]
    acc_dot(nsub - 1)

    @pl.when(j == pl.num_programs(1) - 1)
    def _():
        o_ref[...] = x_ref[...] + mod_ref[5:6, :] * acc_ref[...].T


def peer_experts(h2t, u, vt, nl, c1, r2, e2, x, mod, seq, tm=PEER_TM, te=PEER_TE):
    d, t = h2t.shape
    ne = u.shape[0]
    big = lambda: pl.BlockSpec((PEER_HEADS, PEER_NKEYS, tm), lambda i, j: (0, 0, i))
    nsub = te // PEER_SUB
    return pl.pallas_call(
        _peer_kernel,
        grid=(t // tm, ne // te),
        in_specs=[
            pl.BlockSpec((d, tm), lambda i, j: (0, i)),
            pl.BlockSpec((te, d), lambda i, j: (j, 0)),
            pl.BlockSpec((d, te), lambda i, j: (0, j)),
            big(), big(), big(), big(),
            pl.BlockSpec((tm, d), lambda i, j: (i, 0)),
            pl.BlockSpec((None, 6, d), lambda i, j: ((i * tm) // seq, 0, 0)),
        ],
        out_specs=pl.BlockSpec((tm, d), lambda i, j: (i, 0)),
        out_shape=jax.ShapeDtypeStruct((t, d), F32),
        scratch_shapes=[pltpu.VMEM((nsub, PEER_SUB, tm), F32), pltpu.VMEM((nsub, PEER_SUB, tm), BF16),
                        pltpu.VMEM((d, tm), F32)],
        compiler_params=_cparams(("arbitrary", "arbitrary")),
        name="peer_experts",
    )(h2t, u, vt, nl, c1, r2, e2, x, mod)


def _final_kernel(x_ref, g_ref, o_ref):
    o_ref[...] = _rms_rows(x_ref[...], g_ref[...])


def final_norm(x, g, tm=1024):
    t, d = x.shape
    return pl.pallas_call(
        _final_kernel,
        grid=(t // tm,),
        in_specs=[pl.BlockSpec((tm, d), lambda i: (i, 0)), pl.BlockSpec((1, d), lambda i: (0, 0))],
        out_specs=pl.BlockSpec((tm, d), lambda i: (i, 0)),
        out_shape=jax.ShapeDtypeStruct((t, d), F32),
        compiler_params=_cparams(("arbitrary",)),
        name="final_norm",
    )(x, g)


def _reorder_w_in(w_in):
    o_dt = SSD_WIDTH + XBC_WIDTH
    o_rest = o_dt + SSD_HEADS
    pad = jnp.zeros(w_in.shape[:-1] + (DT_PAD - SSD_HEADS,), w_in.dtype)
    return jnp.concatenate([w_in[..., :o_dt], w_in[..., o_rest:], w_in[..., o_dt:o_rest], pad], axis=-1)


def _pad_heads(p):
    return jnp.pad(p, ((0, 0), (0, DT_PAD - SSD_HEADS)))


def kernel(x, c, ada_w, ada_b, norm1_g, norm2_g, w_in, conv_w, conv_b, dt_bias, a_log, d_skip, ssd_norm_g, moba_norm_g, swa_sinks, swa_norm_g, w_out, peer_wq, peer_k1, peer_k2, peer_u, peer_v, final_g):
    b, s, d = x.shape
    depth = ada_w.shape[0]
    mod_all = ada_mod(c, ada_w, ada_b).reshape(depth, b, 6, d)
    w_in_b = _reorder_w_in(w_in.astype(BF16))
    w_out_b = w_out.astype(BF16)
    wq_b = peer_wq.astype(BF16)
    k1_b = peer_k1.astype(BF16)
    k2_b = peer_k2.astype(BF16)
    dtb = _pad_heads(dt_bias)
    alog = _pad_heads(a_log)
    dskip_x = jnp.repeat(d_skip, HEAD_DIM, axis=-1)
    for l in range(depth):
        mod = mod_all[l]
        z, xbc, dt, mqkv, sqkv = in_proj(x, mod, norm1_g[l][None], w_in_b[l])
        y_ssd = ssd_mixer(z, xbc, dt, conv_w[l], conv_b[l][None], dtb[l][None], alog[l][None],
                          dskip_x[l][None], ssd_norm_g[l][None])
        y_moba = moba_attention(mqkv, moba_norm_g[l][None])
        y_swa = swa_attention(sqkv, swa_sinks[l], swa_norm_g[l][None])
        x, h2, h2t = out_proj(x, y_ssd, y_moba, y_swa, w_out_b[l], mod, norm2_g[l][None])
        nl, c1, r2, e2 = peer_route(h2.reshape(b * s, d), wq_b[l], k1_b[l], k2_b[l])
        x = peer_experts(h2t, peer_u[l].astype(BF16), peer_v[l].T.astype(BF16), nl, c1, r2, e2,
                         x.reshape(b * s, d), mod, s).reshape(b, s, d)
    return final_norm(x.reshape(b * s, d), final_g[None]).reshape(b, s, d)
```

```python
import functools

import numpy as np
import jax
import jax.numpy as jnp
from jax import lax
from jax.experimental import pallas as pl
from jax.experimental.pallas import tpu as pltpu

F32 = jnp.float32
BF16 = jnp.bfloat16
HIGHEST = lax.Precision.HIGHEST

EPS = 1e-5
D_MODEL = 1024
HEAD_DIM = 64
SSD_HEADS = 8
SSD_WIDTH = 512
SSD_GROUPS = 2
SSD_STATE = 128
SSD_CHUNK = 128
CONV_K = 4
XBC_WIDTH = 1024
MOBA_HEADS = 4
MOBA_WIDTH = 256
MOBA_BLOCK = 256
MOBA_TOPK = 3
SWA_HEADS = 4
SWA_KV_HEADS = 2
SWA_WIDTH = 256
SWA_KV_WIDTH = 128
SWA_BLOCK = 128
PEER_HEADS = 8
PEER_TOPK = 16
PEER_NKEYS = 128
PEER_EXPERTS = PEER_NKEYS * PEER_NKEYS
PEER_HALF = 128

DT_PAD = 128
N_IN_PAD = SSD_WIDTH + XBC_WIDTH + 3 * MOBA_WIDTH + SWA_WIDTH + 2 * SWA_KV_WIDTH + DT_PAD
MASKED = -1e30
LOG2E = 1.4426950408889634

VMEM_LIMIT = 56 * 1024 * 1024


def _cparams(sem):
    return pltpu.CompilerParams(dimension_semantics=sem, vmem_limit_bytes=VMEM_LIMIT)


def _dot(a, b, precision=None):
    return jnp.dot(a, b, preferred_element_type=F32, precision=precision)


def _dot_nt(a, b, precision=None):
    return lax.dot_general(a, b, (((1,), (1,)), ((), ())), preferred_element_type=F32,
                           precision=precision)


def _dot_tn(a, b, precision=None):
    return lax.dot_general(a, b, (((0,), (0,)), ((), ())), preferred_element_type=F32,
                           precision=precision)


def _silu(x):
    return x / (1.0 + jnp.exp(-x))


def _rms_rows(x, g):
    ms = jnp.mean(x * x, axis=-1, keepdims=True)
    return x * lax.rsqrt(ms + EPS) * g


def _ada_kernel(c_ref, w_ref, b_ref, o_ref):
    cond = _silu(c_ref[...])
    o_ref[...] = _dot(cond, w_ref[...], HIGHEST) + b_ref[...]


def ada_mod(c, ada_w, ada_b, tn=2048):
    nl, d, n6 = ada_w.shape
    b = c.shape[0]
    return pl.pallas_call(
        _ada_kernel,
        grid=(nl, n6 // tn),
        in_specs=[
            pl.BlockSpec((b, d), lambda l, j: (0, 0)),
            pl.BlockSpec((None, d, tn), lambda l, j: (l, 0, j)),
            pl.BlockSpec((None, 1, tn), lambda l, j: (l, 0, j)),
        ],
        out_specs=pl.BlockSpec((None, b, tn), lambda l, j: (l, 0, j)),
        out_shape=jax.ShapeDtypeStruct((nl, b, n6), F32),
        compiler_params=_cparams(("arbitrary", "arbitrary")),
        name="ada_mod",
    )(c, ada_w, ada_b.reshape(nl, 1, n6))


_O_Z = 0
_O_XBC = _O_Z + SSD_WIDTH
_O_MQKV = _O_XBC + XBC_WIDTH
_O_SQKV = _O_MQKV + 3 * MOBA_WIDTH
_O_DT = _O_SQKV + SWA_WIDTH + 2 * SWA_KV_WIDTH


def _inproj_kernel(x_ref, mod_ref, g_ref, w_ref, z_ref, xbc_ref, dt_ref, mqkv_ref, sqkv_ref):
    h = _rms_rows(x_ref[...], g_ref[...]) * (1.0 + mod_ref[1:2, :]) + mod_ref[0:1, :]
    hb = h.astype(BF16)
    z_ref[...] = _dot(hb, w_ref[:, _O_Z:_O_XBC])
    xbc_ref[...] = _dot(hb, w_ref[:, _O_XBC:_O_MQKV])
    mqkv_ref[...] = _dot(hb, w_ref[:, _O_MQKV:_O_SQKV]).astype(BF16)
    sqkv_ref[...] = _dot(hb, w_ref[:, _O_SQKV:_O_DT]).astype(BF16)
    dt_ref[...] = _dot(hb, w_ref[:, _O_DT:N_IN_PAD])


def in_proj(x, mod, g, w, tm=512):
    b, s, d = x.shape
    tok = lambda n: pl.BlockSpec((None, tm, n), lambda bi, i: (bi, i, 0))
    return pl.pallas_call(
        _inproj_kernel,
        grid=(b, s // tm),
        in_specs=[
            tok(d),
            pl.BlockSpec((None, 6, d), lambda bi, i: (bi, 0, 0)),
            pl.BlockSpec((1, d), lambda bi, i: (0, 0)),
            pl.BlockSpec((d, N_IN_PAD), lambda bi, i: (0, 0)),
        ],
        out_specs=[tok(SSD_WIDTH), tok(XBC_WIDTH), tok(DT_PAD), tok(3 * MOBA_WIDTH),
                   tok(SWA_WIDTH + 2 * SWA_KV_WIDTH)],
        out_shape=[
            jax.ShapeDtypeStruct((b, s, SSD_WIDTH), F32),
            jax.ShapeDtypeStruct((b, s, XBC_WIDTH), F32),
            jax.ShapeDtypeStruct((b, s, DT_PAD), F32),
            jax.ShapeDtypeStruct((b, s, 3 * MOBA_WIDTH), BF16),
            jax.ShapeDtypeStruct((b, s, SWA_WIDTH + 2 * SWA_KV_WIDTH), BF16),
        ],
        compiler_params=_cparams(("arbitrary", "arbitrary")),
        name="in_proj",
    )(x, mod, g, w)


def _ssd_kernel(z_ref, xbc_ref, dt_ref, cw_ref, cb_ref, dtb_ref, alog_ref, dskip_ref, ng_ref,
                o_ref, xpad_ref, state_ref):
    L, N, P = SSD_CHUNK, SSD_STATE, HEAD_DIM
    GW = SSD_WIDTH // SSD_GROUPS
    c = pl.program_id(1)

    @pl.when(c == 0)
    def _():
        xpad_ref[0:8, :] = jnp.zeros((8, XBC_WIDTH), F32)
        state_ref[...] = jnp.zeros_like(state_ref)

    xpad_ref[8:8 + L, :] = xbc_ref[...]
    y = cb_ref[...] + cw_ref[0:1, :] * xpad_ref[5:5 + L, :]
    for k in range(1, CONV_K):
        y = y + cw_ref[k:k + 1, :] * xpad_ref[5 + k:5 + k + L, :]
    xpad_ref[0:8, :] = xpad_ref[L:L + 8, :]
    xc = _silu(y)
    xs = xc[:, :SSD_WIDTH]

    dtr = dt_ref[...] + dtb_ref[...]
    dtv = jnp.maximum(dtr, 0.0) + jnp.log1p(jnp.exp(-jnp.abs(dtr)))
    a = dtv * (-jnp.exp(alog_ref[...]))
    row = lax.broadcasted_iota(jnp.int32, (L, L), 0)
    col = lax.broadcasted_iota(jnp.int32, (L, L), 1)
    causal = col <= row
    a_cs = _dot(causal.astype(F32), a, HIGHEST)
    a_cs_t = a_cs.T
    er = lax.broadcasted_iota(jnp.int32, (DT_PAD, SSD_WIDTH), 0)
    ec = lax.broadcasted_iota(jnp.int32, (DT_PAD, SSD_WIDTH), 1)
    expand = (ec // P == er).astype(F32)
    dt_x = _dot(dtv, expand, HIGHEST)
    acs_x = _dot(a_cs, expand, HIGHEST)
    atot_x = acs_x[L - 1:L, :]

    xdt = xs * dt_x
    w_state = (xdt * jnp.exp(atot_x - acs_x)).astype(BF16)
    decay_in = jnp.exp(acs_x)
    decay_tot = jnp.exp(atot_x)
    lane_head = lax.broadcasted_iota(jnp.int32, (L, GW), 1) // P

    ys = []
    for g in range(SSD_GROUPS):
        bm = xc[:, SSD_WIDTH + g * N:SSD_WIDTH + (g + 1) * N].astype(BF16)
        cm = xc[:, SSD_WIDTH + SSD_GROUPS * N + g * N:SSD_WIDTH + SSD_GROUPS * N + (g + 1) * N].astype(BF16)
        cbm = _dot_nt(cm, bm)
        xdt_g = xdt[:, g * GW:(g + 1) * GW]
        y_g = jnp.zeros((L, GW), F32)
        for r in range(SSD_HEADS // SSD_GROUPS):
            hh = g * (SSD_HEADS // SSD_GROUPS) + r
            seg = jnp.broadcast_to(a_cs[:, hh:hh + 1], (L, L)) - jnp.broadcast_to(a_cs_t[hh:hh + 1, :], (L, L))
            decay = jnp.where(causal, jnp.exp(jnp.where(causal, seg, 0.0)), 0.0)
            m_h = (cbm * decay).astype(BF16)
            x_h = jnp.where(lane_head == r, xdt_g, 0.0).astype(BF16)
            y_g = y_g + _dot(m_h, x_h)
        st = state_ref[g]
        y_g = y_g + decay_in[:, g * GW:(g + 1) * GW] * _dot(cm, st.astype(BF16))
        state_ref[g] = decay_tot[:, g * GW:(g + 1) * GW] * st + _dot_tn(bm, w_state[:, g * GW:(g + 1) * GW])
        ys.append(y_g)
    yfull = jnp.concatenate(ys, axis=-1) + dskip_ref[...] * xs
    gated = yfull * _silu(z_ref[...])
    o_ref[...] = _rms_rows(gated, ng_ref[...]).astype(BF16)


def ssd_mixer(z, xbc, dt, conv_w, conv_b, dt_bias, a_log, d_skip_x, norm_g):
    b, s, _ = z.shape
    L = SSD_CHUNK
    tok = lambda n: pl.BlockSpec((None, L, n), lambda bi, i: (bi, i, 0))
    par = lambda r, n: pl.BlockSpec((r, n), lambda bi, i: (0, 0))
    return pl.pallas_call(
        _ssd_kernel,
        grid=(b, s // L),
        in_specs=[tok(SSD_WIDTH), tok(XBC_WIDTH), tok(DT_PAD), par(CONV_K, XBC_WIDTH), par(1, XBC_WIDTH),
                  par(1, DT_PAD), par(1, DT_PAD), par(1, SSD_WIDTH), par(1, SSD_WIDTH)],
        out_specs=tok(SSD_WIDTH),
        out_shape=jax.ShapeDtypeStruct((b, s, SSD_WIDTH), BF16),
        scratch_shapes=[pltpu.VMEM((L + 8, XBC_WIDTH), F32),
                        pltpu.VMEM((SSD_GROUPS, SSD_STATE, SSD_WIDTH // SSD_GROUPS), F32)],
        compiler_params=_cparams(("arbitrary", "arbitrary")),
        name="ssd_mixer",
    )(z, xbc, dt, conv_w, conv_b, dt_bias, a_log, d_skip_x, norm_g)


def _moba_kernel(q_ref, k_ref, v_ref, ng_ref, o_ref, kmean_ref, vt_ref, qt_ref, selb_ref, sc_ref, p_ref,
                 acc_ref):
    BL, Dh = MOBA_BLOCK, HEAD_DIM
    s = k_ref.shape[0]
    nb = s // BL
    i = pl.program_id(1)

    @pl.when(i == 0)
    def _():
        blk = lax.broadcasted_iota(jnp.int32, (nb, s), 0)
        pos = lax.broadcasted_iota(jnp.int32, (nb, s), 1)
        avg = jnp.where(pos // BL == blk, 1.0 / BL, 0.0).astype(BF16)
        kmean_ref[...] = _dot(avg, k_ref[...])
        for n in range(nb):
            vt_ref[:, n * BL:(n + 1) * BL] = v_ref[n * BL:(n + 1) * BL, :].astype(F32).T.astype(BF16)

    key_i = lax.broadcasted_iota(jnp.int32, (BL, BL), 0)
    qry_i = lax.broadcasted_iota(jnp.int32, (BL, BL), 1)
    own_bias = jnp.where(key_i <= qry_i, 0.0, MASKED)
    past = lax.broadcasted_iota(jnp.int32, (nb, BL), 0) < i
    heads = [slice(h * Dh, (h + 1) * Dh) for h in range(MOBA_HEADS)]

    q_t = (q_ref[...].astype(F32) * (Dh ** -0.5 * LOG2E)).T
    qt_ref[...] = q_t.astype(BF16)
    for h, hs in enumerate(heads):
        gate = _dot(kmean_ref[:, hs], q_t[hs, :], HIGHEST)
        cur = jnp.where(past, gate, -jnp.inf)
        for _ in range(MOBA_TOPK - 1):
            cur = jnp.where(cur == jnp.max(cur, axis=0, keepdims=True), -jnp.inf, cur)
        thr = jnp.max(cur, axis=0, keepdims=True)
        selb_ref[h] = jnp.where(past & (gate >= thr), 0.0, MASKED)

    def score_dot(h, hs, rows, nblk):
        sc_ref[h, :nblk * BL, :] = _dot(k_ref[rows, hs], qt_ref[hs, :])

    def chunk(start, nblk, biases, ms, ls, first, scores_ready, next_start):
        rows = pl.ds(pl.multiple_of(start, BL), nblk * BL)
        next_rows = pl.ds(pl.multiple_of(next_start, BL), 2 * BL)
        if not scores_ready:
            for h, hs in enumerate(heads):
                score_dot(h, hs, rows, nblk)
        ms_new, ls_new, pending = [], [], []
        for h, hs in enumerate(heads):
            scs = []
            m = ms[h]
            for t in range(nblk):
                sc = sc_ref[h, t * BL:(t + 1) * BL, :]
                if biases[h][t].shape[0] == 1:
                    m = jnp.maximum(m, jnp.max(sc, axis=0, keepdims=True) + biases[h][t])
                else:
                    sc = sc + biases[h][t]
                    m = jnp.maximum(m, jnp.max(sc, axis=0, keepdims=True))
                scs.append(sc)
            psum = jnp.zeros((1, BL), F32)
            for t, sc in enumerate(scs):
                shift = m - biases[h][t] if biases[h][t].shape[0] == 1 else m
                p = jnp.exp2(sc - shift)
                psum = psum + jnp.sum(p, axis=0, keepdims=True)
                p_ref[h, t * BL:(t + 1) * BL, :] = p.astype(BF16)
            score_dot(h, hs, next_rows, 2)
            pv = _dot(vt_ref[hs, rows], p_ref[h, :nblk * BL, :])
            alpha = None if first else jnp.exp2(ms[h] - m)
            ls_new.append(psum if first else alpha * ls[h] + psum)
            ms_new.append(m)
            for hsp, pvp, alphap in pending:
                acc_ref[hsp, :] = pvp if first else alphap * acc_ref[hsp, :] + pvp
            pending = [(hs, pv, alpha)]
        for hsp, pvp, alphap in pending:
            acc_ref[hsp, :] = pvp if first else alphap * acc_ref[hsp, :] + pvp
        return tuple(ms_new), tuple(ls_new)

    init = tuple(jnp.full((1, BL), MASKED, F32) for _ in heads)

    def first_single():
        return chunk(i * BL, 1, [[own_bias]] * MOBA_HEADS, init, None, True, False, 0)

    def first_pair():
        biases = [[selb_ref[h, pl.ds(i - 1, 1), :], own_bias] for h in range(MOBA_HEADS)]
        return chunk((i - 1) * BL, 2, biases, init, None, True, False, 0)

    carry = lax.cond(i % 2 == 1, first_pair, first_single)

    def body(n, carry):
        biases = [[selb_ref[h, pl.ds(2 * n, 1), :], selb_ref[h, pl.ds(2 * n + 1, 1), :]]
                  for h in range(MOBA_HEADS)]
        nxt = jnp.minimum(2 * n + 2, nb - 2) * BL
        return chunk(2 * n * BL, 2, biases, carry[0], carry[1], False, True, nxt)

    ms, ls = lax.fori_loop(0, i // 2, body, carry)
    for h, hs in enumerate(heads):
        acc_ref[hs, :] = acc_ref[hs, :] / ls[h]
    o_ref[...] = _rms_rows(acc_ref[...].T, ng_ref[...]).astype(BF16)


def moba_attention(mqkv, norm_g):
    b, s, _ = mqkv.shape
    BL = MOBA_BLOCK
    nb = s // BL
    return pl.pallas_call(
        _moba_kernel,
        grid=(b, nb),
        in_specs=[
            pl.BlockSpec((None, BL, MOBA_WIDTH), lambda bi, i: (bi, i, 0)),
            pl.BlockSpec((None, s, MOBA_WIDTH), lambda bi, i: (bi, 0, 1)),
            pl.BlockSpec((None, s, MOBA_WIDTH), lambda bi, i: (bi, 0, 2)),
            pl.BlockSpec((1, MOBA_WIDTH), lambda bi, i: (0, 0)),
        ],
        out_specs=pl.BlockSpec((None, BL, MOBA_WIDTH), lambda bi, i: (bi, i, 0)),
        out_shape=jax.ShapeDtypeStruct((b, s, MOBA_WIDTH), BF16),
        scratch_shapes=[pltpu.VMEM((nb, MOBA_WIDTH), F32), pltpu.VMEM((MOBA_WIDTH, s), BF16),
                        pltpu.VMEM((MOBA_WIDTH, BL), BF16), pltpu.VMEM((MOBA_HEADS, nb, BL), F32),
                        pltpu.VMEM((MOBA_HEADS, 2 * BL, BL), F32), pltpu.VMEM((MOBA_HEADS, 2 * BL, BL), BF16),
                        pltpu.VMEM((MOBA_WIDTH, BL), F32)],
        compiler_params=_cparams(("arbitrary", "arbitrary")),
        name="moba_attention",
    )(mqkv, mqkv, mqkv, norm_g)


def _swa_kernel(sink_ref, q_ref, kp_ref, kc_ref, vp_ref, vc_ref, ng_ref, o_ref, sc_ref, out_t_ref):
    W, Dh = SWA_BLOCK, HEAD_DIM
    i = pl.program_id(1)
    key_i = lax.broadcasted_iota(jnp.int32, (2 * W, W), 0)
    qry_i = lax.broadcasted_iota(jnp.int32, (2 * W, W), 1)
    rel = qry_i + W - key_i
    ok = (rel >= 0) & (rel < W) & ((key_i >= W) | (i > 0))
    bias = jnp.where(ok, 0.0, MASKED)
    rep = SWA_HEADS // SWA_KV_HEADS
    q_t = (q_ref[...].astype(F32) * (Dh ** -0.5 * LOG2E)).T.astype(BF16)
    v_bands = []
    for g in range(SWA_KV_HEADS):
        gs = slice(g * Dh, (g + 1) * Dh)
        k_band = jnp.concatenate([kp_ref[:, gs], kc_ref[:, gs]], axis=0)
        v_bands.append(jnp.concatenate([vp_ref[:, gs], vc_ref[:, gs]], axis=0))
        for r in range(rep):
            hq = g * rep + r
            sc_ref[hq] = _dot(k_band, q_t[hq * Dh:(hq + 1) * Dh, :])
    for hq in range(SWA_HEADS):
        hs = slice(hq * Dh, (hq + 1) * Dh)
        sc = sc_ref[hq] + bias
        sink = sink_ref[hq] * LOG2E
        m = jnp.maximum(jnp.max(sc, axis=0, keepdims=True), sink)
        p = jnp.exp2(sc - m)
        den = jnp.sum(p, axis=0, keepdims=True) + jnp.exp2(sink - m)
        out_t_ref[hs, :] = _dot_tn(v_bands[hq // rep], p.astype(BF16)) / den
    o_ref[...] = _rms_rows(out_t_ref[...].T, ng_ref[...]).astype(BF16)


def swa_attention(sqkv, sinks, norm_g):
    b, s, _ = sqkv.shape
    W = SWA_BLOCK
    prev = lambda bi, i: jnp.maximum(i - 1, 0)
    return pl.pallas_call(
        _swa_kernel,
        grid=(b, s // W),
        in_specs=[
            pl.BlockSpec(memory_space=pltpu.SMEM),
            pl.BlockSpec((None, W, SWA_WIDTH), lambda bi, i: (bi, i, 0)),
            pl.BlockSpec((None, W, SWA_KV_WIDTH), lambda bi, i: (bi, prev(bi, i), 2)),
            pl.BlockSpec((None, W, SWA_KV_WIDTH), lambda bi, i: (bi, i, 2)),
            pl.BlockSpec((None, W, SWA_KV_WIDTH), lambda bi, i: (bi, prev(bi, i), 3)),
            pl.BlockSpec((None, W, SWA_KV_WIDTH), lambda bi, i: (bi, i, 3)),
            pl.BlockSpec((1, SWA_WIDTH), lambda bi, i: (0, 0)),
        ],
        out_specs=pl.BlockSpec((None, W, SWA_WIDTH), lambda bi, i: (bi, i, 0)),
        out_shape=jax.ShapeDtypeStruct((b, s, SWA_WIDTH), BF16),
        scratch_shapes=[pltpu.VMEM((SWA_HEADS, 2 * W, W), F32), pltpu.VMEM((SWA_WIDTH, W), F32)],
        compiler_params=_cparams(("arbitrary", "arbitrary")),
        name="swa_attention",
    )(sinks, sqkv, sqkv, sqkv, sqkv, sqkv, norm_g)


def _outproj_kernel(x_ref, ys_ref, ym_ref, yw_ref, w_ref, mod_ref, g_ref, xo_ref, h_ref, ht_ref):
    ycat = jnp.concatenate([ys_ref[...], ym_ref[...], yw_ref[...]], axis=-1)
    xn = x_ref[...] + mod_ref[2:3, :] * _dot(ycat, w_ref[...])
    xo_ref[...] = xn
    h = _rms_rows(xn, g_ref[...]) * (1.0 + mod_ref[4:5, :]) + mod_ref[3:4, :]
    h_ref[...] = h.astype(BF16)
    ht_ref[...] = h.T.astype(BF16)


def out_proj(x, y_ssd, y_moba, y_swa, w_out, mod, g, tm=512):
    b, s, d = x.shape
    nt = s // tm
    tok = lambda n: pl.BlockSpec((None, tm, n), lambda bi, i: (bi, i, 0))
    return pl.pallas_call(
        _outproj_kernel,
        grid=(b, nt),
        in_specs=[tok(d), tok(SSD_WIDTH), tok(MOBA_WIDTH), tok(SWA_WIDTH),
                  pl.BlockSpec((d, d), lambda bi, i: (0, 0)),
                  pl.BlockSpec((None, 6, d), lambda bi, i: (bi, 0, 0)),
                  pl.BlockSpec((1, d), lambda bi, i: (0, 0))],
        out_specs=[tok(d), tok(d), pl.BlockSpec((d, tm), lambda bi, i: (0, bi * nt + i))],
        out_shape=[jax.ShapeDtypeStruct((b, s, d), F32), jax.ShapeDtypeStruct((b, s, d), BF16),
                   jax.ShapeDtypeStruct((d, b * s), BF16)],
        compiler_params=_cparams(("arbitrary", "arbitrary")),
        name="out_proj",
    )(x, y_ssd, y_moba, y_swa, w_out, mod, g)


_CAND = [(r, c) for r in range(PEER_TOPK) for c in range(PEER_TOPK) if (r + 1) * (c + 1) <= PEER_TOPK]
_CAND_ROWS = -(-len(_CAND) // 8) * 8
_NL_SPLIT = 4
_NL_TOP = PEER_TOPK // (_NL_SPLIT + 1)


def _oddeven_merge_sort_pairs(n):
    pairs = []
    p = 1
    while p < n:
        k = p
        while k >= 1:
            for j in range(k % p, n - k, 2 * k):
                for i in range(min(k, n - j - k)):
                    if (i + j) // (2 * p) == (i + j + k) // (2 * p):
                        pairs.append((i + j, i + j + k))
            k //= 2
        p *= 2
    return pairs


_SUBLANES = 8
_SORT16 = _oddeven_merge_sort_pairs(PEER_TOPK)


def _exchange(vals, x, y):
    vals[x], vals[y] = jnp.maximum(vals[x], vals[y]), jnp.minimum(vals[x], vals[y])


def _tiles(s_t):
    r = s_t.reshape(s_t.shape[0] // _SUBLANES, _SUBLANES, s_t.shape[1])
    return [r[k] for k in range(r.shape[0])]


def _top16_sorted(s_t):
    vals = _tiles(s_t)
    for x, y in _SORT16:
        _exchange(vals, x, y)
    shift = _SUBLANES // 2
    while shift >= 1:
        other = [pltpu.roll(v, shift, 0) for v in vals]
        vals = [jnp.maximum(vals[k], other[PEER_TOPK - 1 - k]) for k in range(PEER_TOPK)]
        d = PEER_TOPK // 2
        while d >= 1:
            for k in range(PEER_TOPK):
                if k & d == 0:
                    _exchange(vals, k, k + d)
            d //= 2
        shift //= 2
    return vals


def _rank_in(x, top):
    def pick(bits, lo, hi):
        if not bits:
            return top[(lo + hi) // 2]
        mid = (lo + hi) // 2
        return jnp.where(bits[0], pick(bits[1:], mid + 1, hi), pick(bits[1:], lo, mid - 1))

    bits = []
    rank = jnp.zeros(x.shape, F32)
    step = PEER_TOPK // 2
    while step >= 1:
        below = x < pick(bits, 0, PEER_TOPK - 2)
        rank = rank + jnp.where(below, float(step), 0.0)
        bits.append(below)
        step //= 2
    return jnp.where(x < top[PEER_TOPK - 1], float(PEER_TOPK), rank)


_ROUTE_LANES = 128


def _route_kernel(h_ref, wq_ref, k1_ref, k2_ref, nl_ref, c1_ref, r2_ref, e2_ref, cand_ref):
    tm = h_ref.shape[0]
    q = _dot(h_ref[...], wq_ref[...])
    k1 = k1_ref[...]
    k2 = k2_ref[...]
    cand_ref[len(_CAND):, :] = jnp.full((_CAND_ROWS - len(_CAND), _ROUTE_LANES), -jnp.inf, F32)
    for h in range(PEER_HEADS):
        q1 = q[:, (2 * h) * PEER_HALF:(2 * h + 1) * PEER_HALF].astype(BF16)
        q2 = q[:, (2 * h + 1) * PEER_HALF:(2 * h + 2) * PEER_HALF].astype(BF16)
        s1_all = _dot_nt(k1, q1)
        s2_all = _dot_nt(k2, q2)
        for lt in range(tm // _ROUTE_LANES):
            lanes = slice(lt * _ROUTE_LANES, (lt + 1) * _ROUTE_LANES)
            s1 = s1_all[:, lanes]
            s2 = s2_all[:, lanes]
            t1 = _top16_sorted(s1)
            t2 = _top16_sorted(s2)
            v1 = [t[0:1, :] for t in t1]
            v2 = [t[0:1, :] for t in t2]
            rank2 = jnp.concatenate([_rank_in(x, t2) for x in _tiles(s2)], axis=0)
            for idx, (r, c) in enumerate(_CAND):
                cand_ref[idx:idx + 1, :] = v1[r] + v2[c]
            cand = cand_ref[...]
            cur = cand
            for _ in range(PEER_TOPK - 1):
                cur = jnp.where(cur == jnp.max(cur, axis=0, keepdims=True), -jnp.inf, cur)
            theta = jnp.max(cur, axis=0, keepdims=True)
            top = v1[0] + v2[0]
            zsum = jnp.sum(jnp.where(cand >= theta, jnp.exp(cand - top), 0.0), axis=0, keepdims=True)
            nlim = jnp.zeros(s1.shape, F32)
            for c in range(_NL_SPLIT):
                nlim = nlim + jnp.where(s1 + v2[c] >= theta, 1.0, 0.0)
            extra = jnp.zeros(s1.shape, F32)
            for r in reversed(range(_NL_TOP)):
                tail = jnp.zeros_like(theta)
                for c in range(_NL_SPLIT, PEER_TOPK):
                    if (r + 1) * (c + 1) <= PEER_TOPK:
                        tail = tail + jnp.where(v1[r] + v2[c] >= theta, 1.0, 0.0)
                extra = jnp.where(s1 == v1[r], tail, extra)
            nl_ref[h, :, lanes] = nlim + extra
            c1_ref[h, :, lanes] = jnp.exp(s1 - v1[0]) * (0.5 / zsum)
            r2_ref[h, :, lanes] = rank2.astype(BF16)
            e2_ref[h, :, lanes] = jnp.exp(s2 - v2[0]).astype(BF16)


def peer_route(h2, wq, k1, k2, tm=256):
    t, d = h2.shape
    big = lambda: pl.BlockSpec((PEER_HEADS, PEER_NKEYS, tm), lambda i: (0, 0, i))
    bigs = lambda dt: jax.ShapeDtypeStruct((PEER_HEADS, PEER_NKEYS, t), dt)
    return pl.pallas_call(
        _route_kernel,
        grid=(t // tm,),
        in_specs=[
            pl.BlockSpec((tm, d), lambda i: (i, 0)),
            pl.BlockSpec((d, 2 * PEER_HEADS * PEER_HALF), lambda i: (0, 0)),
            pl.BlockSpec((PEER_NKEYS, PEER_HALF), lambda i: (0, 0)),
            pl.BlockSpec((PEER_NKEYS, PEER_HALF), lambda i: (0, 0)),
        ],
        out_specs=[big(), big(), big(), big()],
        out_shape=[bigs(F32), bigs(F32), bigs(BF16), bigs(BF16)],
        scratch_shapes=[pltpu.VMEM((_CAND_ROWS, _ROUTE_LANES), F32)],
        compiler_params=_cparams(("arbitrary",)),
        name="peer_route",
    )(h2, wq, k1, k2)


_SLAB = 16
_LANES = 256
PEER_TM = 512
PEER_TE = 2048
PEER_SUB = 512
_PRE_AHEAD = 2


def _peer_kernel(ht_ref, u_ref, vt_ref, nl_ref, c1_ref, r2_ref, e2_ref, x_ref, mod_ref,
                 o_ref, pre_ref, act_ref, acc_ref):
    j = pl.program_id(1)
    tm = ht_ref.shape[1]
    te = u_ref.shape[0]
    groups = PEER_SUB // PEER_NKEYS

    @pl.when(j == 0)
    def _():
        acc_ref[...] = jnp.zeros_like(acc_ref)

    nsub = te // PEER_SUB
    sub_rows = [slice(sc * PEER_SUB, (sc + 1) * PEER_SUB) for sc in range(nsub)]
    def acc_dot(sc):
        acc_ref[...] = acc_ref[...] + _dot(vt_ref[:, sub_rows[sc]], act_ref[sc])

    def pre_dot(sc):
        pre_ref[sc] = _dot(u_ref[sub_rows[sc], :], ht_ref[...])

    for sc in range(min(_PRE_AHEAD, nsub)):
        pre_dot(sc)
    for sc in range(nsub):
        if sc + _PRE_AHEAD < nsub:
            pre_dot(sc + _PRE_AHEAD)
        if sc >= 1:
            acc_dot(sc - 1)
        for gi in range(groups):
            local = sc * groups + gi
            base8 = pl.multiple_of(j * (te // PEER_NKEYS) + (local // 8) * 8, 8)
            r8 = local % 8
            for lt in range(tm // _LANES):
                lanes = slice(lt * _LANES, (lt + 1) * _LANES)
                nslab = PEER_NKEYS // _SLAB
                slabs = [slice(sl * _SLAB, (sl + 1) * _SLAB) for sl in range(nslab)]
                gsum = [jnp.zeros((_SLAB, _LANES), BF16)] * nslab
                for h in range(PEER_HEADS):
                    nl_row = jnp.broadcast_to(nl_ref[h, pl.ds(base8, 8), lanes][r8:r8 + 1, :],
                                              (_SLAB, _LANES)).astype(BF16)
                    c1_row = jnp.broadcast_to(c1_ref[h, pl.ds(base8, 8), lanes][r8:r8 + 1, :],
                                              (_SLAB, _LANES)).astype(BF16)
                    for sl, rows in enumerate(slabs):
                        hit = r2_ref[h, rows, lanes] < nl_row
                        gsum[sl] = gsum[sl] + jnp.where(hit, e2_ref[h, rows, lanes], jnp.zeros((), BF16)) * c1_row
                for sl in range(nslab):
                    prow = slice(gi * PEER_NKEYS + sl * _SLAB, gi * PEER_NKEYS + (sl + 1) * _SLAB)
                    pre = pre_ref[sc, prow, lanes]
                    gelu2 = pre * (1.0 + lax.erf(pre * (2.0 ** -0.5)))
                    act_ref[sc, prow, lanes] = gelu2.astype(BF16) * gsum[sl]
    acc_dot(nsub - 1)

    @pl.when(j == pl.num_programs(1) - 1)
    def _():
        o_ref[...] = x_ref[...] + mod_ref[5:6, :] * acc_ref[...].T


def peer_experts(h2t, u, vt, nl, c1, r2, e2, x, mod, seq, tm=PEER_TM, te=PEER_TE):
    d, t = h2t.shape
    ne = u.shape[0]
    big = lambda: pl.BlockSpec((PEER_HEADS, PEER_NKEYS, tm), lambda i, j: (0, 0, i))
    nsub = te // PEER_SUB
    return pl.pallas_call(
        _peer_kernel,
        grid=(t // tm, ne // te),
        in_specs=[
            pl.BlockSpec((d, tm), lambda i, j: (0, i)),
            pl.BlockSpec((te, d), lambda i, j: (j, 0)),
            pl.BlockSpec((d, te), lambda i, j: (0, j)),
            big(), big(), big(), big(),
            pl.BlockSpec((tm, d), lambda i, j: (i, 0)),
            pl.BlockSpec((None, 6, d), lambda i, j: ((i * tm) // seq, 0, 0)),
        ],
        out_specs=pl.BlockSpec((tm, d), lambda i, j: (i, 0)),
        out_shape=jax.ShapeDtypeStruct((t, d), F32),
        scratch_shapes=[pltpu.VMEM((nsub, PEER_SUB, tm), F32), pltpu.VMEM((nsub, PEER_SUB, tm), BF16),
                        pltpu.VMEM((d, tm), F32)],
        compiler_params=_cparams(("arbitrary", "arbitrary")),
        name="peer_experts",
    )(h2t, u, vt, nl, c1, r2, e2, x, mod)


def _final_kernel(x_ref, g_ref, o_ref):
    o_ref[...] = _rms_rows(x_ref[...], g_ref[...])


def final_norm(x, g, tm=1024):
    t, d = x.shape
    return pl.pallas_call(
        _final_kernel,
        grid=(t // tm,),
        in_specs=[pl.BlockSpec((tm, d), lambda i: (i, 0)), pl.BlockSpec((1, d), lambda i: (0, 0))],
        out_specs=pl.BlockSpec((tm, d), lambda i: (i, 0)),
        out_shape=jax.ShapeDtypeStruct((t, d), F32),
        compiler_params=_cparams(("arbitrary",)),
        name="final_norm",
    )(x, g)


def _reorder_w_in(w_in):
    o_dt = SSD_WIDTH + XBC_WIDTH
    o_rest = o_dt + SSD_HEADS
    pad = jnp.zeros(w_in.shape[:-1] + (DT_PAD - SSD_HEADS,), w_in.dtype)
    return jnp.concatenate([w_in[..., :o_dt], w_in[..., o_rest:], w_in[..., o_dt:o_rest], pad], axis=-1)


def _pad_heads(p):
    return jnp.pad(p, ((0, 0), (0, DT_PAD - SSD_HEADS)))


def kernel(x, c, ada_w, ada_b, norm1_g, norm2_g, w_in, conv_w, conv_b, dt_bias, a_log, d_skip, ssd_norm_g, moba_norm_g, swa_sinks, swa_norm_g, w_out, peer_wq, peer_k1, peer_k2, peer_u, peer_v, final_g):
    b, s, d = x.shape
    depth = ada_w.shape[0]
    mod_all = ada_mod(c, ada_w, ada_b).reshape(depth, b, 6, d)
    w_in_b = _reorder_w_in(w_in.astype(BF16))
    w_out_b = w_out.astype(BF16)
    wq_b = peer_wq.astype(BF16)
    k1_b = peer_k1.astype(BF16)
    k2_b = peer_k2.astype(BF16)
    dtb = _pad_heads(dt_bias)
    alog = _pad_heads(a_log)
    dskip_x = jnp.repeat(d_skip, HEAD_DIM, axis=-1)
    for l in range(depth):
        mod = mod_all[l]
        z, xbc, dt, mqkv, sqkv = in_proj(x, mod, norm1_g[l][None], w_in_b[l])
        y_ssd = ssd_mixer(z, xbc, dt, conv_w[l], conv_b[l][None], dtb[l][None], alog[l][None],
                          dskip_x[l][None], ssd_norm_g[l][None])
        y_moba = moba_attention(mqkv, moba_norm_g[l][None])
        y_swa = swa_attention(sqkv, swa_sinks[l], swa_norm_g[l][None])
        x, h2, h2t = out_proj(x, y_ssd, y_moba, y_swa, w_out_b[l], mod, norm2_g[l][None])
        nl, c1, r2, e2 = peer_route(h2.reshape(b * s, d), wq_b[l], k1_b[l], k2_b[l])
        x = peer_experts(h2t, peer_u[l].astype(BF16), peer_v[l].T.astype(BF16), nl, c1, r2, e2,
                         x.reshape(b * s, d), mod, s).reshape(b, s, d)
    return final_norm(x.reshape(b * s, d), final_g[None]).reshape(b, s, d)
```

```python
import functools

import numpy as np
import jax
import jax.numpy as jnp
from jax import lax
from jax.experimental import pallas as pl
from jax.experimental.pallas import tpu as pltpu

F32 = jnp.float32
BF16 = jnp.bfloat16
HIGHEST = lax.Precision.HIGHEST

EPS = 1e-5
D_MODEL = 1024
HEAD_DIM = 64
SSD_HEADS = 8
SSD_WIDTH = 512
SSD_GROUPS = 2
SSD_STATE = 128
SSD_CHUNK = 128
CONV_K = 4
XBC_WIDTH = 1024
MOBA_HEADS = 4
MOBA_WIDTH = 256
MOBA_BLOCK = 256
MOBA_TOPK = 3
SWA_HEADS = 4
SWA_KV_HEADS = 2
SWA_WIDTH = 256
SWA_KV_WIDTH = 128
SWA_BLOCK = 128
PEER_HEADS = 8
PEER_TOPK = 16
PEER_NKEYS = 128
PEER_EXPERTS = PEER_NKEYS * PEER_NKEYS
PEER_HALF = 128

DT_PAD = 128
N_IN_PAD = SSD_WIDTH + XBC_WIDTH + 3 * MOBA_WIDTH + SWA_WIDTH + 2 * SWA_KV_WIDTH + DT_PAD
MASKED = -1e30
LOG2E = 1.4426950408889634

VMEM_LIMIT = 56 * 1024 * 1024


def _cparams(sem):
    return pltpu.CompilerParams(dimension_semantics=sem, vmem_limit_bytes=VMEM_LIMIT)


def _dot(a, b, precision=None):
    return jnp.dot(a, b, preferred_element_type=F32, precision=precision)


def _dot_nt(a, b, precision=None):
    return lax.dot_general(a, b, (((1,), (1,)), ((), ())), preferred_element_type=F32,
                           precision=precision)


def _dot_tn(a, b, precision=None):
    return lax.dot_general(a, b, (((0,), (0,)), ((), ())), preferred_element_type=F32,
                           precision=precision)


def _silu(x):
    return x / (1.0 + jnp.exp(-x))


def _rms_rows(x, g):
    ms = jnp.mean(x * x, axis=-1, keepdims=True)
    return x * lax.rsqrt(ms + EPS) * g


def _ada_kernel(c_ref, w_ref, b_ref, o_ref):
    cond = _silu(c_ref[...])
    o_ref[...] = _dot(cond, w_ref[...], HIGHEST) + b_ref[...]


def ada_mod(c, ada_w, ada_b, tn=2048):
    nl, d, n6 = ada_w.shape
    b = c.shape[0]
    return pl.pallas_call(
        _ada_kernel,
        grid=(nl, n6 // tn),
        in_specs=[
            pl.BlockSpec((b, d), lambda l, j: (0, 0)),
            pl.BlockSpec((None, d, tn), lambda l, j: (l, 0, j)),
            pl.BlockSpec((None, 1, tn), lambda l, j: (l, 0, j)),
        ],
        out_specs=pl.BlockSpec((None, b, tn), lambda l, j: (l, 0, j)),
        out_shape=jax.ShapeDtypeStruct((nl, b, n6), F32),
        compiler_params=_cparams(("arbitrary", "arbitrary")),
        name="ada_mod",
    )(c, ada_w, ada_b.reshape(nl, 1, n6))


_O_Z = 0
_O_XBC = _O_Z + SSD_WIDTH
_O_MQKV = _O_XBC + XBC_WIDTH
_O_SQKV = _O_MQKV + 3 * MOBA_WIDTH
_O_DT = _O_SQKV + SWA_WIDTH + 2 * SWA_KV_WIDTH


def _inproj_kernel(x_ref, mod_ref, g_ref, w_ref, z_ref, xbc_ref, dt_ref, mqkv_ref, sqkv_ref):
    h = _rms_rows(x_ref[...], g_ref[...]) * (1.0 + mod_ref[1:2, :]) + mod_ref[0:1, :]
    hb = h.astype(BF16)
    z_ref[...] = _dot(hb, w_ref[:, _O_Z:_O_XBC])
    xbc_ref[...] = _dot(hb, w_ref[:, _O_XBC:_O_MQKV])
    mqkv_ref[...] = _dot(hb, w_ref[:, _O_MQKV:_O_SQKV]).astype(BF16)
    sqkv_ref[...] = _dot(hb, w_ref[:, _O_SQKV:_O_DT]).astype(BF16)
    dt_ref[...] = _dot(hb, w_ref[:, _O_DT:N_IN_PAD])


def in_proj(x, mod, g, w, tm=512):
    b, s, d = x.shape
    tok = lambda n: pl.BlockSpec((None, tm, n), lambda bi, i: (bi, i, 0))
    return pl.pallas_call(
        _inproj_kernel,
        grid=(b, s // tm),
        in_specs=[
            tok(d),
            pl.BlockSpec((None, 6, d), lambda bi, i: (bi, 0, 0)),
            pl.BlockSpec((1, d), lambda bi, i: (0, 0)),
            pl.BlockSpec((d, N_IN_PAD), lambda bi, i: (0, 0)),
        ],
        out_specs=[tok(SSD_WIDTH), tok(XBC_WIDTH), tok(DT_PAD), tok(3 * MOBA_WIDTH),
                   tok(SWA_WIDTH + 2 * SWA_KV_WIDTH)],
        out_shape=[
            jax.ShapeDtypeStruct((b, s, SSD_WIDTH), F32),
            jax.ShapeDtypeStruct((b, s, XBC_WIDTH), F32),
            jax.ShapeDtypeStruct((b, s, DT_PAD), F32),
            jax.ShapeDtypeStruct((b, s, 3 * MOBA_WIDTH), BF16),
            jax.ShapeDtypeStruct((b, s, SWA_WIDTH + 2 * SWA_KV_WIDTH), BF16),
        ],
        compiler_params=_cparams(("arbitrary", "arbitrary")),
        name="in_proj",
    )(x, mod, g, w)


def _ssd_kernel(z_ref, xbc_ref, dt_ref, cw_ref, cb_ref, dtb_ref, alog_ref, dskip_ref, ng_ref,
                o_ref, xpad_ref, state_ref):
    L, N, P = SSD_CHUNK, SSD_STATE, HEAD_DIM
    GW = SSD_WIDTH // SSD_GROUPS
    c = pl.program_id(1)

    @pl.when(c == 0)
    def _():
        xpad_ref[0:8, :] = jnp.zeros((8, XBC_WIDTH), F32)
        state_ref[...] = jnp.zeros_like(state_ref)

    xpad_ref[8:8 + L, :] = xbc_ref[...]
    y = cb_ref[...] + cw_ref[0:1, :] * xpad_ref[5:5 + L, :]
    for k in range(1, CONV_K):
        y = y + cw_ref[k:k + 1, :] * xpad_ref[5 + k:5 + k + L, :]
    xpad_ref[0:8, :] = xpad_ref[L:L + 8, :]
    xc = _silu(y)
    xs = xc[:, :SSD_WIDTH]

    dtr = dt_ref[...] + dtb_ref[...]
    dtv = jnp.maximum(dtr, 0.0) + jnp.log1p(jnp.exp(-jnp.abs(dtr)))
    a = dtv * (-jnp.exp(alog_ref[...]))
    row = lax.broadcasted_iota(jnp.int32, (L, L), 0)
    col = lax.broadcasted_iota(jnp.int32, (L, L), 1)
    causal = col <= row
    a_cs = _dot(causal.astype(F32), a, HIGHEST)
    a_cs_t = a_cs.T
    er = lax.broadcasted_iota(jnp.int32, (DT_PAD, SSD_WIDTH), 0)
    ec = lax.broadcasted_iota(jnp.int32, (DT_PAD, SSD_WIDTH), 1)
    expand = (ec // P == er).astype(F32)
    dt_x = _dot(dtv, expand, HIGHEST)
    acs_x = _dot(a_cs, expand, HIGHEST)
    atot_x = acs_x[L - 1:L, :]

    xdt = xs * dt_x
    w_state = (xdt * jnp.exp(atot_x - acs_x)).astype(BF16)
    decay_in = jnp.exp(acs_x)
    decay_tot = jnp.exp(atot_x)
    lane_head = lax.broadcasted_iota(jnp.int32, (L, GW), 1) // P

    ys = []
    for g in range(SSD_GROUPS):
        bm = xc[:, SSD_WIDTH + g * N:SSD_WIDTH + (g + 1) * N].astype(BF16)
        cm = xc[:, SSD_WIDTH + SSD_GROUPS * N + g * N:SSD_WIDTH + SSD_GROUPS * N + (g + 1) * N].astype(BF16)
        cbm = _dot_nt(cm, bm)
        xdt_g = xdt[:, g * GW:(g + 1) * GW]
        y_g = jnp.zeros((L, GW), F32)
        for r in range(SSD_HEADS // SSD_GROUPS):
            hh = g * (SSD_HEADS // SSD_GROUPS) + r
            seg = jnp.broadcast_to(a_cs[:, hh:hh + 1], (L, L)) - jnp.broadcast_to(a_cs_t[hh:hh + 1, :], (L, L))
            decay = jnp.where(causal, jnp.exp(seg), 0.0)
            m_h = (cbm * decay).astype(BF16)
            x_h = jnp.where(lane_head == r, xdt_g, 0.0).astype(BF16)
            y_g = y_g + _dot(m_h, x_h)
        st = state_ref[g]
        y_g = y_g + decay_in[:, g * GW:(g + 1) * GW] * _dot(cm, st.astype(BF16))
        state_ref[g] = decay_tot[:, g * GW:(g + 1) * GW] * st + _dot_tn(bm, w_state[:, g * GW:(g + 1) * GW])
        ys.append(y_g)
    yfull = jnp.concatenate(ys, axis=-1) + dskip_ref[...] * xs
    gated = yfull * _silu(z_ref[...])
    o_ref[...] = _rms_rows(gated, ng_ref[...]).astype(BF16)


def ssd_mixer(z, xbc, dt, conv_w, conv_b, dt_bias, a_log, d_skip_x, norm_g):
    b, s, _ = z.shape
    L = SSD_CHUNK
    tok = lambda n: pl.BlockSpec((None, L, n), lambda bi, i: (bi, i, 0))
    par = lambda r, n: pl.BlockSpec((r, n), lambda bi, i: (0, 0))
    return pl.pallas_call(
        _ssd_kernel,
        grid=(b, s // L),
        in_specs=[tok(SSD_WIDTH), tok(XBC_WIDTH), tok(DT_PAD), par(CONV_K, XBC_WIDTH), par(1, XBC_WIDTH),
                  par(1, DT_PAD), par(1, DT_PAD), par(1, SSD_WIDTH), par(1, SSD_WIDTH)],
        out_specs=tok(SSD_WIDTH),
        out_shape=jax.ShapeDtypeStruct((b, s, SSD_WIDTH), BF16),
        scratch_shapes=[pltpu.VMEM((L + 8, XBC_WIDTH), F32),
                        pltpu.VMEM((SSD_GROUPS, SSD_STATE, SSD_WIDTH // SSD_GROUPS), F32)],
        compiler_params=_cparams(("arbitrary", "arbitrary")),
        name="ssd_mixer",
    )(z, xbc, dt, conv_w, conv_b, dt_bias, a_log, d_skip_x, norm_g)


def _moba_kernel(q_ref, k_ref, v_ref, ng_ref, o_ref, kmean_ref, vt_ref, qt_ref, selb_ref, sc_ref, p_ref,
                 acc_ref):
    BL, Dh = MOBA_BLOCK, HEAD_DIM
    s = k_ref.shape[0]
    nb = s // BL
    i = pl.program_id(1)

    @pl.when(i == 0)
    def _():
        blk = lax.broadcasted_iota(jnp.int32, (nb, s), 0)
        pos = lax.broadcasted_iota(jnp.int32, (nb, s), 1)
        avg = jnp.where(pos // BL == blk, 1.0 / BL, 0.0).astype(BF16)
        kmean_ref[...] = _dot(avg, k_ref[...])
        for n in range(nb):
            vt_ref[:, n * BL:(n + 1) * BL] = v_ref[n * BL:(n + 1) * BL, :].astype(F32).T.astype(BF16)

    key_i = lax.broadcasted_iota(jnp.int32, (BL, BL), 0)
    qry_i = lax.broadcasted_iota(jnp.int32, (BL, BL), 1)
    own_bias = jnp.where(key_i <= qry_i, 0.0, MASKED)
    past = lax.broadcasted_iota(jnp.int32, (nb, BL), 0) < i
    heads = [slice(h * Dh, (h + 1) * Dh) for h in range(MOBA_HEADS)]

    q_t = (q_ref[...].astype(F32) * (Dh ** -0.5 * LOG2E)).T
    qt_ref[...] = q_t.astype(BF16)
    for h, hs in enumerate(heads):
        gate = _dot(kmean_ref[:, hs], q_t[hs, :], HIGHEST)
        cur = jnp.where(past, gate, -jnp.inf)
        for _ in range(MOBA_TOPK - 1):
            cur = jnp.where(cur == jnp.max(cur, axis=0, keepdims=True), -jnp.inf, cur)
        thr = jnp.max(cur, axis=0, keepdims=True)
        selb_ref[h] = jnp.where(past & (gate >= thr), 0.0, MASKED)

    def score_dot(h, hs, rows, nblk):
        sc_ref[h, :nblk * BL, :] = _dot(k_ref[rows, hs], qt_ref[hs, :])

    def chunk(start, nblk, biases, ms, ls, first, scores_ready, next_start):
        rows = pl.ds(pl.multiple_of(start, BL), nblk * BL)
        next_rows = pl.ds(pl.multiple_of(next_start, BL), 2 * BL)
        if not scores_ready:
            for h, hs in enumerate(heads):
                score_dot(h, hs, rows, nblk)
        ms_new, ls_new, pending = [], [], []
        for h, hs in enumerate(heads):
            scs = []
            m = ms[h]
            for t in range(nblk):
                sc = sc_ref[h, t * BL:(t + 1) * BL, :]
                if biases[h][t].shape[0] == 1:
                    m = jnp.maximum(m, jnp.max(sc, axis=0, keepdims=True) + biases[h][t])
                else:
                    sc = sc + biases[h][t]
                    m = jnp.maximum(m, jnp.max(sc, axis=0, keepdims=True))
                scs.append(sc)
            psum = jnp.zeros((1, BL), F32)
            for t, sc in enumerate(scs):
                shift = m - biases[h][t] if biases[h][t].shape[0] == 1 else m
                p = jnp.exp2(sc - shift)
                psum = psum + jnp.sum(p, axis=0, keepdims=True)
                p_ref[h, t * BL:(t + 1) * BL, :] = p.astype(BF16)
            score_dot(h, hs, next_rows, 2)
            pv = _dot(vt_ref[hs, rows], p_ref[h, :nblk * BL, :])
            alpha = None if first else jnp.exp2(ms[h] - m)
            ls_new.append(psum if first else alpha * ls[h] + psum)
            ms_new.append(m)
            for hsp, pvp, alphap in pending:
                acc_ref[hsp, :] = pvp if first else alphap * acc_ref[hsp, :] + pvp
            pending = [(hs, pv, alpha)]
        for hsp, pvp, alphap in pending:
            acc_ref[hsp, :] = pvp if first else alphap * acc_ref[hsp, :] + pvp
        return tuple(ms_new), tuple(ls_new)

    init = tuple(jnp.full((1, BL), MASKED, F32) for _ in heads)

    def first_single():
        return chunk(i * BL, 1, [[own_bias]] * MOBA_HEADS, init, None, True, False, 0)

    def first_pair():
        biases = [[selb_ref[h, pl.ds(i - 1, 1), :], own_bias] for h in range(MOBA_HEADS)]
        return chunk((i - 1) * BL, 2, biases, init, None, True, False, 0)

    carry = lax.cond(i % 2 == 1, first_pair, first_single)

    def body(n, carry):
        biases = [[selb_ref[h, pl.ds(2 * n, 1), :], selb_ref[h, pl.ds(2 * n + 1, 1), :]]
                  for h in range(MOBA_HEADS)]
        nxt = jnp.minimum(2 * n + 2, nb - 2) * BL
        return chunk(2 * n * BL, 2, biases, carry[0], carry[1], False, True, nxt)

    ms, ls = lax.fori_loop(0, i // 2, body, carry)
    for h, hs in enumerate(heads):
        acc_ref[hs, :] = acc_ref[hs, :] / ls[h]
    o_ref[...] = _rms_rows(acc_ref[...].T, ng_ref[...]).astype(BF16)


def moba_attention(mqkv, norm_g):
    b, s, _ = mqkv.shape
    BL = MOBA_BLOCK
    nb = s // BL
    return pl.pallas_call(
        _moba_kernel,
        grid=(b, nb),
        in_specs=[
            pl.BlockSpec((None, BL, MOBA_WIDTH), lambda bi, i: (bi, i, 0)),
            pl.BlockSpec((None, s, MOBA_WIDTH), lambda bi, i: (bi, 0, 1)),
            pl.BlockSpec((None, s, MOBA_WIDTH), lambda bi, i: (bi, 0, 2)),
            pl.BlockSpec((1, MOBA_WIDTH), lambda bi, i: (0, 0)),
        ],
        out_specs=pl.BlockSpec((None, BL, MOBA_WIDTH), lambda bi, i: (bi, i, 0)),
        out_shape=jax.ShapeDtypeStruct((b, s, MOBA_WIDTH), BF16),
        scratch_shapes=[pltpu.VMEM((nb, MOBA_WIDTH), F32), pltpu.VMEM((MOBA_WIDTH, s), BF16),
                        pltpu.VMEM((MOBA_WIDTH, BL), BF16), pltpu.VMEM((MOBA_HEADS, nb, BL), F32),
                        pltpu.VMEM((MOBA_HEADS, 2 * BL, BL), F32), pltpu.VMEM((MOBA_HEADS, 2 * BL, BL), BF16),
                        pltpu.VMEM((MOBA_WIDTH, BL), F32)],
        compiler_params=_cparams(("arbitrary", "arbitrary")),
        name="moba_attention",
    )(mqkv, mqkv, mqkv, norm_g)


SWA_QBLOCKS = 4


def _swa_kernel(sink_ref, q_ref, kp_ref, kc_ref, vp_ref, vc_ref, ng_ref, o_ref, sc_ref, out_t_ref):
    W, Dh, NQ = SWA_BLOCK, HEAD_DIM, SWA_QBLOCKS
    i = pl.program_id(1)
    key_i = lax.broadcasted_iota(jnp.int32, (2 * W, W), 0)
    qry_i = lax.broadcasted_iota(jnp.int32, (2 * W, W), 1)
    rel = qry_i + W - key_i
    in_win = (rel >= 0) & (rel < W)
    bias_rest = jnp.where(in_win, 0.0, MASKED)
    bias_first = jnp.where(in_win & ((key_i >= W) | (i > 0)), 0.0, MASKED)
    rep = SWA_HEADS // SWA_KV_HEADS
    q_t = (q_ref[...].astype(F32) * (Dh ** -0.5 * LOG2E)).T.astype(BF16)

    def band(prev_ref, cur_ref, t, gs):
        if t == 0:
            return jnp.concatenate([prev_ref[:, gs], cur_ref[0:W, gs]], axis=0)
        return cur_ref[(t - 1) * W:(t + 1) * W, gs]

    for t in range(NQ):
        for g in range(SWA_KV_HEADS):
            k_band = band(kp_ref, kc_ref, t, slice(g * Dh, (g + 1) * Dh))
            for r in range(rep):
                hq = g * rep + r
                sc_ref[t * SWA_HEADS + hq] = _dot(k_band, q_t[hq * Dh:(hq + 1) * Dh, t * W:(t + 1) * W])
    for t in range(NQ):
        for hq in range(SWA_HEADS):
            g = hq // rep
            hs = slice(hq * Dh, (hq + 1) * Dh)
            sc = sc_ref[t * SWA_HEADS + hq] + (bias_first if t == 0 else bias_rest)
            sink = sink_ref[hq] * LOG2E
            m = jnp.maximum(jnp.max(sc, axis=0, keepdims=True), sink)
            p = jnp.exp2(sc - m)
            den = jnp.sum(p, axis=0, keepdims=True) + jnp.exp2(sink - m)
            v_band = band(vp_ref, vc_ref, t, slice(g * Dh, (g + 1) * Dh))
            out_t_ref[hs, t * W:(t + 1) * W] = _dot_tn(v_band, p.astype(BF16)) / den
    o_ref[...] = _rms_rows(out_t_ref[...].T, ng_ref[...]).astype(BF16)


def swa_attention(sqkv, sinks, norm_g):
    b, s, _ = sqkv.shape
    W, NQ = SWA_BLOCK, SWA_QBLOCKS
    tq = NQ * W
    prev = lambda bi, i: jnp.maximum(i * NQ - 1, 0)
    return pl.pallas_call(
        _swa_kernel,
        grid=(b, s // tq),
        in_specs=[
            pl.BlockSpec(memory_space=pltpu.SMEM),
            pl.BlockSpec((None, tq, SWA_WIDTH), lambda bi, i: (bi, i, 0)),
            pl.BlockSpec((None, W, SWA_KV_WIDTH), lambda bi, i: (bi, prev(bi, i), 2)),
            pl.BlockSpec((None, tq, SWA_KV_WIDTH), lambda bi, i: (bi, i, 2)),
            pl.BlockSpec((None, W, SWA_KV_WIDTH), lambda bi, i: (bi, prev(bi, i), 3)),
            pl.BlockSpec((None, tq, SWA_KV_WIDTH), lambda bi, i: (bi, i, 3)),
            pl.BlockSpec((1, SWA_WIDTH), lambda bi, i: (0, 0)),
        ],
        out_specs=pl.BlockSpec((None, tq, SWA_WIDTH), lambda bi, i: (bi, i, 0)),
        out_shape=jax.ShapeDtypeStruct((b, s, SWA_WIDTH), BF16),
        scratch_shapes=[pltpu.VMEM((NQ * SWA_HEADS, 2 * W, W), F32), pltpu.VMEM((SWA_WIDTH, tq), F32)],
        compiler_params=_cparams(("arbitrary", "arbitrary")),
        name="swa_attention",
    )(sinks, sqkv, sqkv, sqkv, sqkv, sqkv, norm_g)


def _outproj_kernel(x_ref, ys_ref, ym_ref, yw_ref, w_ref, mod_ref, g_ref, xo_ref, h_ref, ht_ref):
    ycat = jnp.concatenate([ys_ref[...], ym_ref[...], yw_ref[...]], axis=-1)
    xn = x_ref[...] + mod_ref[2:3, :] * _dot(ycat, w_ref[...])
    xo_ref[...] = xn
    h = _rms_rows(xn, g_ref[...]) * (1.0 + mod_ref[4:5, :]) + mod_ref[3:4, :]
    h_ref[...] = h.astype(BF16)
    ht_ref[...] = h.T.astype(BF16)


def out_proj(x, y_ssd, y_moba, y_swa, w_out, mod, g, tm=512):
    b, s, d = x.shape
    nt = s // tm
    tok = lambda n: pl.BlockSpec((None, tm, n), lambda bi, i: (bi, i, 0))
    return pl.pallas_call(
        _outproj_kernel,
        grid=(b, nt),
        in_specs=[tok(d), tok(SSD_WIDTH), tok(MOBA_WIDTH), tok(SWA_WIDTH),
                  pl.BlockSpec((d, d), lambda bi, i: (0, 0)),
                  pl.BlockSpec((None, 6, d), lambda bi, i: (bi, 0, 0)),
                  pl.BlockSpec((1, d), lambda bi, i: (0, 0))],
        out_specs=[tok(d), tok(d), pl.BlockSpec((d, tm), lambda bi, i: (0, bi * nt + i))],
        out_shape=[jax.ShapeDtypeStruct((b, s, d), F32), jax.ShapeDtypeStruct((b, s, d), BF16),
                   jax.ShapeDtypeStruct((d, b * s), BF16)],
        compiler_params=_cparams(("arbitrary", "arbitrary")),
        name="out_proj",
    )(x, y_ssd, y_moba, y_swa, w_out, mod, g)


_CAND = [(r, c) for r in range(PEER_TOPK) for c in range(PEER_TOPK) if (r + 1) * (c + 1) <= PEER_TOPK]
_CAND_ROWS = -(-len(_CAND) // 8) * 8
_NL_SPLIT = 4
_NL_TOP = PEER_TOPK // (_NL_SPLIT + 1)


def _oddeven_merge_sort_pairs(n):
    pairs = []
    p = 1
    while p < n:
        k = p
        while k >= 1:
            for j in range(k % p, n - k, 2 * k):
                for i in range(min(k, n - j - k)):
                    if (i + j) // (2 * p) == (i + j + k) // (2 * p):
                        pairs.append((i + j, i + j + k))
            k //= 2
        p *= 2
    return pairs


_SUBLANES = 8
_SORT16 = _oddeven_merge_sort_pairs(PEER_TOPK)


def _exchange(vals, x, y):
    vals[x], vals[y] = jnp.maximum(vals[x], vals[y]), jnp.minimum(vals[x], vals[y])


def _tiles(s_t):
    r = s_t.reshape(s_t.shape[0] // _SUBLANES, _SUBLANES, s_t.shape[1])
    return [r[k] for k in range(r.shape[0])]


def _top16_sorted(s_t):
    vals = _tiles(s_t)
    for x, y in _SORT16:
        _exchange(vals, x, y)
    shift = _SUBLANES // 2
    while shift >= 1:
        other = [pltpu.roll(v, shift, 0) for v in vals]
        vals = [jnp.maximum(vals[k], other[PEER_TOPK - 1 - k]) for k in range(PEER_TOPK)]
        d = PEER_TOPK // 2
        while d >= 1:
            for k in range(PEER_TOPK):
                if k & d == 0:
                    _exchange(vals, k, k + d)
            d //= 2
        shift //= 2
    return vals


def _rank_in(x, top):
    def pick(bits, lo, hi):
        if not bits:
            return top[(lo + hi) // 2]
        mid = (lo + hi) // 2
        return jnp.where(bits[0], pick(bits[1:], mid + 1, hi), pick(bits[1:], lo, mid - 1))

    bits = []
    rank = jnp.zeros(x.shape, F32)
    step = PEER_TOPK // 2
    while step >= 1:
        below = x < pick(bits, 0, PEER_TOPK - 2)
        rank = rank + jnp.where(below, float(step), 0.0)
        bits.append(below)
        step //= 2
    return jnp.where(x < top[PEER_TOPK - 1], float(PEER_TOPK), rank)


_ROUTE_LANES = 128


def _route_kernel(h_ref, wq_ref, k1_ref, k2_ref, nl_ref, c1_ref, r2_ref, e2_ref, cand_ref):
    tm = h_ref.shape[0]
    q = _dot(h_ref[...], wq_ref[...])
    k1 = k1_ref[...]
    k2 = k2_ref[...]
    cand_ref[len(_CAND):, :] = jnp.full((_CAND_ROWS - len(_CAND), _ROUTE_LANES), -jnp.inf, F32)
    for h in range(PEER_HEADS):
        q1 = q[:, (2 * h) * PEER_HALF:(2 * h + 1) * PEER_HALF].astype(BF16)
        q2 = q[:, (2 * h + 1) * PEER_HALF:(2 * h + 2) * PEER_HALF].astype(BF16)
        s1_all = _dot_nt(k1, q1)
        s2_all = _dot_nt(k2, q2)
        for lt in range(tm // _ROUTE_LANES):
            lanes = slice(lt * _ROUTE_LANES, (lt + 1) * _ROUTE_LANES)
            s1 = s1_all[:, lanes]
            s2 = s2_all[:, lanes]
            t1 = _top16_sorted(s1)
            t2 = _top16_sorted(s2)
            v1 = [t[0:1, :] for t in t1]
            v2 = [t[0:1, :] for t in t2]
            rank2 = jnp.concatenate([_rank_in(x, t2) for x in _tiles(s2)], axis=0)
            for idx, (r, c) in enumerate(_CAND):
                cand_ref[idx:idx + 1, :] = v1[r] + v2[c]
            cand = cand_ref[...]
            cur = cand
            for _ in range(PEER_TOPK - 1):
                cur = jnp.where(cur == jnp.max(cur, axis=0, keepdims=True), -jnp.inf, cur)
            theta = jnp.max(cur, axis=0, keepdims=True)
            top = v1[0] + v2[0]
            zsum = jnp.sum(jnp.where(cand >= theta, jnp.exp(cand - top), 0.0), axis=0, keepdims=True)
            nlim = jnp.zeros(s1.shape, F32)
            for c in range(_NL_SPLIT):
                nlim = nlim + jnp.where(s1 + v2[c] >= theta, 1.0, 0.0)
            extra = jnp.zeros(s1.shape, F32)
            for r in reversed(range(_NL_TOP)):
                tail = jnp.zeros_like(theta)
                for c in range(_NL_SPLIT, PEER_TOPK):
                    if (r + 1) * (c + 1) <= PEER_TOPK:
                        tail = tail + jnp.where(v1[r] + v2[c] >= theta, 1.0, 0.0)
                extra = jnp.where(s1 == v1[r], tail, extra)
            nl_ref[h, :, lanes] = nlim + extra
            c1_ref[h, :, lanes] = jnp.exp(s1 - v1[0]) * (0.5 / zsum)
            r2_ref[h, :, lanes] = rank2.astype(BF16)
            e2_ref[h, :, lanes] = jnp.exp(s2 - v2[0]).astype(BF16)


def peer_route(h2, wq, k1, k2, tm=256):
    t, d = h2.shape
    big = lambda: pl.BlockSpec((PEER_HEADS, PEER_NKEYS, tm), lambda i: (0, 0, i))
    bigs = lambda dt: jax.ShapeDtypeStruct((PEER_HEADS, PEER_NKEYS, t), dt)
    return pl.pallas_call(
        _route_kernel,
        grid=(t // tm,),
        in_specs=[
            pl.BlockSpec((tm, d), lambda i: (i, 0)),
            pl.BlockSpec((d, 2 * PEER_HEADS * PEER_HALF), lambda i: (0, 0)),
            pl.BlockSpec((PEER_NKEYS, PEER_HALF), lambda i: (0, 0)),
            pl.BlockSpec((PEER_NKEYS, PEER_HALF), lambda i: (0, 0)),
        ],
        out_specs=[big(), big(), big(), big()],
        out_shape=[bigs(F32), bigs(F32), bigs(BF16), bigs(BF16)],
        scratch_shapes=[pltpu.VMEM((_CAND_ROWS, _ROUTE_LANES), F32)],
        compiler_params=_cparams(("arbitrary",)),
        name="peer_route",
    )(h2, wq, k1, k2)


_SLAB = 16
_LANES = 256
PEER_TM = 512
PEER_TE = 2048
PEER_SUB = 512
_PRE_AHEAD = 2


def _peer_kernel(ht_ref, u_ref, vt_ref, nl_ref, c1_ref, r2_ref, e2_ref, x_ref, mod_ref,
                 o_ref, pre_ref, act_ref, acc_ref):
    j = pl.program_id(1)
    tm = ht_ref.shape[1]
    te = u_ref.shape[0]
    groups = PEER_SUB // PEER_NKEYS

    @pl.when(j == 0)
    def _():
        acc_ref[...] = jnp.zeros_like(acc_ref)

    nsub = te // PEER_SUB
    sub_rows = [slice(sc * PEER_SUB, (sc + 1) * PEER_SUB) for sc in range(nsub)]
    def acc_dot(sc):
        acc_ref[...] = acc_ref[...] + _dot(vt_ref[:, sub_rows[sc]], act_ref[sc])

    def pre_dot(sc):
        pre_ref[sc] = _dot(u_ref[sub_rows[sc], :], ht_ref[...])

    for sc in range(min(_PRE_AHEAD, nsub)):
        pre_dot(sc)
    for sc in range(nsub):
        if sc + _PRE_AHEAD < nsub:
            pre_dot(sc + _PRE_AHEAD)
        if sc >= 1:
            acc_dot(sc - 1)
        for gi in range(groups):
            local = sc * groups + gi
            base8 = pl.multiple_of(j * (te // PEER_NKEYS) + (local // 8) * 8, 8)
            r8 = local % 8
            for lt in range(tm // _LANES):
                lanes = slice(lt * _LANES, (lt + 1) * _LANES)
                nslab = PEER_NKEYS // _SLAB
                slabs = [slice(sl * _SLAB, (sl + 1) * _SLAB) for sl in range(nslab)]
                gsum = [jnp.zeros((_SLAB, _LANES), BF16)] * nslab
                for h in range(PEER_HEADS):
                    nl_row = jnp.broadcast_to(nl_ref[h, pl.ds(base8, 8), lanes][r8:r8 + 1, :],
                                              (_SLAB, _LANES)).astype(BF16)
                    c1_row = jnp.broadcast_to(c1_ref[h, pl.ds(base8, 8), lanes][r8:r8 + 1, :],
                                              (_SLAB, _LANES)).astype(BF16)
                    for sl, rows in enumerate(slabs):
                        hit = r2_ref[h, rows, lanes] < nl_row
                        gsum[sl] = gsum[sl] + jnp.where(hit, e2_ref[h, rows, lanes], jnp.zeros((), BF16)) * c1_row
                for sl in range(nslab):
                    prow = slice(gi * PEER_NKEYS + sl * _SLAB, gi * PEER_NKEYS + (sl + 1) * _SLAB)
                    pre = pre_ref[sc, prow, lanes]
                    gelu2 = pre * (1.0 + lax.erf(pre * (2.0 ** -0.5)))
                    act_ref[sc, prow, lanes] = gelu2.astype(BF16) * gsum[sl]
    acc_dot(nsub - 1)

    @pl.when(j == pl.num_programs(1) - 1)
    def _():
        o_ref[...] = x_ref[...] + mod_ref[5:6, :] * acc_ref[...].T


def peer_experts(h2t, u, vt, nl, c1, r2, e2, x, mod, seq, tm=PEER_TM, te=PEER_TE):
    d, t = h2t.shape
    ne = u.shape[0]
    big = lambda: pl.BlockSpec((PEER_HEADS, PEER_NKEYS, tm), lambda i, j: (0, 0, i))
    nsub = te // PEER_SUB
    return pl.pallas_call(
        _peer_kernel,
        grid=(t // tm, ne // te),
        in_specs=[
            pl.BlockSpec((d, tm), lambda i, j: (0, i)),
            pl.BlockSpec((te, d), lambda i, j: (j, 0)),
            pl.BlockSpec((d, te), lambda i, j: (0, j)),
            big(), big(), big(), big(),
            pl.BlockSpec((tm, d), lambda i, j: (i, 0)),
            pl.BlockSpec((None, 6, d), lambda i, j: ((i * tm) // seq, 0, 0)),
        ],
        out_specs=pl.BlockSpec((tm, d), lambda i, j: (i, 0)),
        out_shape=jax.ShapeDtypeStruct((t, d), F32),
        scratch_shapes=[pltpu.VMEM((nsub, PEER_SUB, tm), F32), pltpu.VMEM((nsub, PEER_SUB, tm), BF16),
                        pltpu.VMEM((d, tm), F32)],
        compiler_params=_cparams(("arbitrary", "arbitrary")),
        name="peer_experts",
    )(h2t, u, vt, nl, c1, r2, e2, x, mod)


def _final_kernel(x_ref, g_ref, o_ref):
    o_ref[...] = _rms_rows(x_ref[...], g_ref[...])


def final_norm(x, g, tm=1024):
    t, d = x.shape
    return pl.pallas_call(
        _final_kernel,
        grid=(t // tm,),
        in_specs=[pl.BlockSpec((tm, d), lambda i: (i, 0)), pl.BlockSpec((1, d), lambda i: (0, 0))],
        out_specs=pl.BlockSpec((tm, d), lambda i: (i, 0)),
        out_shape=jax.ShapeDtypeStruct((t, d), F32),
        compiler_params=_cparams(("arbitrary",)),
        name="final_norm",
    )(x, g)


def _reorder_w_in(w_in):
    o_dt = SSD_WIDTH + XBC_WIDTH
    o_rest = o_dt + SSD_HEADS
    pad = jnp.zeros(w_in.shape[:-1] + (DT_PAD - SSD_HEADS,), w_in.dtype)
    return jnp.concatenate([w_in[..., :o_dt], w_in[..., o_rest:], w_in[..., o_dt:o_rest], pad], axis=-1)


def _pad_heads(p):
    return jnp.pad(p, ((0, 0), (0, DT_PAD - SSD_HEADS)))


def kernel(x, c, ada_w, ada_b, norm1_g, norm2_g, w_in, conv_w, conv_b, dt_bias, a_log, d_skip, ssd_norm_g, moba_norm_g, swa_sinks, swa_norm_g, w_out, peer_wq, peer_k1, peer_k2, peer_u, peer_v, final_g):
    b, s, d = x.shape
    depth = ada_w.shape[0]
    mod_all = ada_mod(c, ada_w, ada_b).reshape(depth, b, 6, d)
    w_in_b = _reorder_w_in(w_in.astype(BF16))
    w_out_b = w_out.astype(BF16)
    wq_b = peer_wq.astype(BF16)
    k1_b = peer_k1.astype(BF16)
    k2_b = peer_k2.astype(BF16)
    dtb = _pad_heads(dt_bias)
    alog = _pad_heads(a_log)
    dskip_x = jnp.repeat(d_skip, HEAD_DIM, axis=-1)
    for l in range(depth):
        mod = mod_all[l]
        z, xbc, dt, mqkv, sqkv = in_proj(x, mod, norm1_g[l][None], w_in_b[l])
        y_ssd = ssd_mixer(z, xbc, dt, conv_w[l], conv_b[l][None], dtb[l][None], alog[l][None],
                          dskip_x[l][None], ssd_norm_g[l][None])
        y_moba = moba_attention(mqkv, moba_norm_g[l][None])
        y_swa = swa_attention(sqkv, swa_sinks[l], swa_norm_g[l][None])
        x, h2, h2t = out_proj(x, y_ssd, y_moba, y_swa, w_out_b[l], mod, norm2_g[l][None])
        nl, c1, r2, e2 = peer_route(h2.reshape(b * s, d), wq_b[l], k1_b[l], k2_b[l])
        x = peer_experts(h2t, peer_u[l].astype(BF16), peer_v[l].T.astype(BF16), nl, c1, r2, e2,
                         x.reshape(b * s, d), mod, s).reshape(b, s, d)
    return final_norm(x.reshape(b * s, d), final_g[None]).reshape(b, s, d)
```

```python
import functools

import numpy as np
import jax
import jax.numpy as jnp
from jax import lax
from jax.experimental import pallas as pl
from jax.experimental.pallas import tpu as pltpu

F32 = jnp.float32
BF16 = jnp.bfloat16
HIGHEST = lax.Precision.HIGHEST

EPS = 1e-5
D_MODEL = 1024
HEAD_DIM = 64
SSD_HEADS = 8
SSD_WIDTH = 512
SSD_GROUPS = 2
SSD_STATE = 128
SSD_CHUNK = 128
CONV_K = 4
XBC_WIDTH = 1024
MOBA_HEADS = 4
MOBA_WIDTH = 256
MOBA_BLOCK = 256
MOBA_TOPK = 3
SWA_HEADS = 4
SWA_KV_HEADS = 2
SWA_WIDTH = 256
SWA_KV_WIDTH = 128
SWA_BLOCK = 128
PEER_HEADS = 8
PEER_TOPK = 16
PEER_NKEYS = 128
PEER_EXPERTS = PEER_NKEYS * PEER_NKEYS
PEER_HALF = 128

DT_PAD = 128
N_IN_PAD = SSD_WIDTH + XBC_WIDTH + 3 * MOBA_WIDTH + SWA_WIDTH + 2 * SWA_KV_WIDTH + DT_PAD
MASKED = -1e30
LOG2E = 1.4426950408889634

VMEM_LIMIT = 56 * 1024 * 1024


def _cparams(sem):
    return pltpu.CompilerParams(dimension_semantics=sem, vmem_limit_bytes=VMEM_LIMIT)


def _dot(a, b, precision=None):
    return jnp.dot(a, b, preferred_element_type=F32, precision=precision)


def _dot_nt(a, b, precision=None):
    return lax.dot_general(a, b, (((1,), (1,)), ((), ())), preferred_element_type=F32,
                           precision=precision)


def _dot_tn(a, b, precision=None):
    return lax.dot_general(a, b, (((0,), (0,)), ((), ())), preferred_element_type=F32,
                           precision=precision)


def _split3(x):
    hi = x.astype(BF16)
    rest = x - hi.astype(F32)
    mid = rest.astype(BF16)
    lo = (rest - mid.astype(F32)).astype(BF16)
    return hi, mid, lo


def _silu(x):
    return x / (1.0 + jnp.exp(-x))


def _rms_rows(x, g):
    ms = jnp.mean(x * x, axis=-1, keepdims=True)
    return x * lax.rsqrt(ms + EPS) * g


def _ada_kernel(c_ref, w_ref, b_ref, o_ref):
    cond = _silu(c_ref[...])
    o_ref[...] = _dot(cond, w_ref[...], HIGHEST) + b_ref[...]


def ada_mod(c, ada_w, ada_b, tn=2048):
    nl, d, n6 = ada_w.shape
    b = c.shape[0]
    return pl.pallas_call(
        _ada_kernel,
        grid=(nl, n6 // tn),
        in_specs=[
            pl.BlockSpec((b, d), lambda l, j: (0, 0)),
            pl.BlockSpec((None, d, tn), lambda l, j: (l, 0, j)),
            pl.BlockSpec((None, 1, tn), lambda l, j: (l, 0, j)),
        ],
        out_specs=pl.BlockSpec((None, b, tn), lambda l, j: (l, 0, j)),
        out_shape=jax.ShapeDtypeStruct((nl, b, n6), F32),
        compiler_params=_cparams(("arbitrary", "arbitrary")),
        name="ada_mod",
    )(c, ada_w, ada_b.reshape(nl, 1, n6))


_O_Z = 0
_O_XBC = _O_Z + SSD_WIDTH
_O_MQKV = _O_XBC + XBC_WIDTH
_O_SQKV = _O_MQKV + 3 * MOBA_WIDTH
_O_DT = _O_SQKV + SWA_WIDTH + 2 * SWA_KV_WIDTH


def _inproj_kernel(x_ref, mod_ref, g_ref, w_ref, z_ref, xbc_ref, dt_ref, mqkv_ref, sqkv_ref):
    h = _rms_rows(x_ref[...], g_ref[...]) * (1.0 + mod_ref[1:2, :]) + mod_ref[0:1, :]
    hb = h.astype(BF16)
    z_ref[...] = _dot(hb, w_ref[:, _O_Z:_O_XBC])
    xbc_ref[...] = _dot(hb, w_ref[:, _O_XBC:_O_MQKV])
    mqkv_ref[...] = _dot(hb, w_ref[:, _O_MQKV:_O_SQKV]).astype(BF16)
    sqkv_ref[...] = _dot(hb, w_ref[:, _O_SQKV:_O_DT]).astype(BF16)
    dt_ref[...] = _dot(hb, w_ref[:, _O_DT:N_IN_PAD])


def in_proj(x, mod, g, w, tm=512):
    b, s, d = x.shape
    tok = lambda n: pl.BlockSpec((None, tm, n), lambda bi, i: (bi, i, 0))
    return pl.pallas_call(
        _inproj_kernel,
        grid=(b, s // tm),
        in_specs=[
            tok(d),
            pl.BlockSpec((None, 6, d), lambda bi, i: (bi, 0, 0)),
            pl.BlockSpec((1, d), lambda bi, i: (0, 0)),
            pl.BlockSpec((d, N_IN_PAD), lambda bi, i: (0, 0)),
        ],
        out_specs=[tok(SSD_WIDTH), tok(XBC_WIDTH), tok(DT_PAD), tok(3 * MOBA_WIDTH),
                   tok(SWA_WIDTH + 2 * SWA_KV_WIDTH)],
        out_shape=[
            jax.ShapeDtypeStruct((b, s, SSD_WIDTH), F32),
            jax.ShapeDtypeStruct((b, s, XBC_WIDTH), F32),
            jax.ShapeDtypeStruct((b, s, DT_PAD), F32),
            jax.ShapeDtypeStruct((b, s, 3 * MOBA_WIDTH), BF16),
            jax.ShapeDtypeStruct((b, s, SWA_WIDTH + 2 * SWA_KV_WIDTH), BF16),
        ],
        compiler_params=_cparams(("arbitrary", "arbitrary")),
        name="in_proj",
    )(x, mod, g, w)


def _ssd_kernel(z_ref, xbc_ref, dt_ref, cw_ref, cb_ref, dtb_ref, alog_ref, dskip_ref, ng_ref,
                o_ref, xpad_ref, state_ref):
    L, N, P = SSD_CHUNK, SSD_STATE, HEAD_DIM
    GW = SSD_WIDTH // SSD_GROUPS
    c = pl.program_id(1)

    @pl.when(c == 0)
    def _():
        xpad_ref[0:8, :] = jnp.zeros((8, XBC_WIDTH), F32)
        state_ref[...] = jnp.zeros_like(state_ref)

    xpad_ref[8:8 + L, :] = xbc_ref[...]
    tiles = xpad_ref[...].reshape((L + 8) // 8, 8, XBC_WIDTH)
    sub = lax.broadcasted_iota(jnp.int32, (L // 8, 8, XBC_WIDTH), 1)
    y = cb_ref[...] + cw_ref[CONV_K - 1:CONV_K, :] * xbc_ref[...]
    for s in range(1, CONV_K):
        rolled = pltpu.roll(tiles, s, 1)
        delayed = jnp.where(sub < s, rolled[:L // 8], rolled[1:]).reshape(L, XBC_WIDTH)
        y = y + cw_ref[CONV_K - 1 - s:CONV_K - s, :] * delayed
    xpad_ref[0:8, :] = xpad_ref[L:L + 8, :]
    xc = _silu(y)
    xs = xc[:, :SSD_WIDTH]

    dtr = dt_ref[...] + dtb_ref[...]
    dtv = jnp.maximum(dtr, 0.0) + jnp.log1p(jnp.exp(-jnp.abs(dtr)))
    a = dtv * (-jnp.exp(alog_ref[...]))
    row = lax.broadcasted_iota(jnp.int32, (L, L), 0)
    col = lax.broadcasted_iota(jnp.int32, (L, L), 1)
    causal = col <= row
    tri = jnp.where(causal, 1.0, 0.0).astype(BF16)
    a_cs = sum(_dot(tri, part) for part in _split3(a))
    a_cs_t = a_cs.T
    er = lax.broadcasted_iota(jnp.int32, (DT_PAD, SSD_WIDTH), 0)
    ec = lax.broadcasted_iota(jnp.int32, (DT_PAD, SSD_WIDTH), 1)
    expand = jnp.where(ec // P == er, 1.0, 0.0).astype(BF16)
    dt_x = sum(_dot(part, expand) for part in _split3(dtv))
    acs_x = sum(_dot(part, expand) for part in _split3(a_cs))
    atot_x = acs_x[L - 1:L, :]

    xdt = xs * dt_x
    w_state = (xdt * jnp.exp(atot_x - acs_x)).astype(BF16)
    decay_in = jnp.exp(acs_x)
    decay_tot = jnp.exp(atot_x)
    lane_head = lax.broadcasted_iota(jnp.int32, (L, GW), 1) // P

    ys = []
    for g in range(SSD_GROUPS):
        bm = xc[:, SSD_WIDTH + g * N:SSD_WIDTH + (g + 1) * N].astype(BF16)
        cm = xc[:, SSD_WIDTH + SSD_GROUPS * N + g * N:SSD_WIDTH + SSD_GROUPS * N + (g + 1) * N].astype(BF16)
        cbm = _dot_nt(cm, bm)
        xdt_g = xdt[:, g * GW:(g + 1) * GW]
        y_g = jnp.zeros((L, GW), F32)
        for r in range(SSD_HEADS // SSD_GROUPS):
            hh = g * (SSD_HEADS // SSD_GROUPS) + r
            seg = jnp.broadcast_to(a_cs[:, hh:hh + 1], (L, L)) - jnp.broadcast_to(a_cs_t[hh:hh + 1, :], (L, L))
            decay = jnp.where(causal, jnp.exp(seg), 0.0)
            m_h = (cbm * decay).astype(BF16)
            x_h = jnp.where(lane_head == r, xdt_g, 0.0).astype(BF16)
            y_g = y_g + _dot(m_h, x_h)
        st = state_ref[g]
        y_g = y_g + decay_in[:, g * GW:(g + 1) * GW] * _dot(cm, st.astype(BF16))
        state_ref[g] = decay_tot[:, g * GW:(g + 1) * GW] * st + _dot_tn(bm, w_state[:, g * GW:(g + 1) * GW])
        ys.append(y_g)
    yfull = jnp.concatenate(ys, axis=-1) + dskip_ref[...] * xs
    gated = yfull * _silu(z_ref[...])
    o_ref[...] = _rms_rows(gated, ng_ref[...]).astype(BF16)


def ssd_mixer(z, xbc, dt, conv_w, conv_b, dt_bias, a_log, d_skip_x, norm_g):
    b, s, _ = z.shape
    L = SSD_CHUNK
    tok = lambda n: pl.BlockSpec((None, L, n), lambda bi, i: (bi, i, 0))
    par = lambda r, n: pl.BlockSpec((r, n), lambda bi, i: (0, 0))
    return pl.pallas_call(
        _ssd_kernel,
        grid=(b, s // L),
        in_specs=[tok(SSD_WIDTH), tok(XBC_WIDTH), tok(DT_PAD), par(CONV_K, XBC_WIDTH), par(1, XBC_WIDTH),
                  par(1, DT_PAD), par(1, DT_PAD), par(1, SSD_WIDTH), par(1, SSD_WIDTH)],
        out_specs=tok(SSD_WIDTH),
        out_shape=jax.ShapeDtypeStruct((b, s, SSD_WIDTH), BF16),
        scratch_shapes=[pltpu.VMEM((L + 8, XBC_WIDTH), F32),
                        pltpu.VMEM((SSD_GROUPS, SSD_STATE, SSD_WIDTH // SSD_GROUPS), F32)],
        compiler_params=_cparams(("arbitrary", "arbitrary")),
        name="ssd_mixer",
    )(z, xbc, dt, conv_w, conv_b, dt_bias, a_log, d_skip_x, norm_g)


def _moba_kernel(q_ref, k_ref, v_ref, ng_ref, o_ref, kmean_ref, vt_ref, qt_ref, selb_ref, sc_ref, p_ref,
                 acc_ref):
    BL, Dh = MOBA_BLOCK, HEAD_DIM
    s = k_ref.shape[0]
    nb = s // BL
    i = pl.program_id(1)

    @pl.when(i == 0)
    def _():
        blk = lax.broadcasted_iota(jnp.int32, (nb, s), 0)
        pos = lax.broadcasted_iota(jnp.int32, (nb, s), 1)
        avg = jnp.where(pos // BL == blk, 1.0 / BL, 0.0).astype(BF16)
        kmean_ref[...] = _dot(avg, k_ref[...])
        for n in range(nb):
            vt_ref[:, n * BL:(n + 1) * BL] = v_ref[n * BL:(n + 1) * BL, :].astype(F32).T.astype(BF16)

    key_i = lax.broadcasted_iota(jnp.int32, (BL, BL), 0)
    qry_i = lax.broadcasted_iota(jnp.int32, (BL, BL), 1)
    own_bias = jnp.where(key_i <= qry_i, 0.0, MASKED)
    past = lax.broadcasted_iota(jnp.int32, (nb, BL), 0) < i
    heads = [slice(h * Dh, (h + 1) * Dh) for h in range(MOBA_HEADS)]

    q_t = (q_ref[...].astype(F32) * (Dh ** -0.5 * LOG2E)).T
    qt_ref[...] = q_t.astype(BF16)
    for h, hs in enumerate(heads):
        gate = _dot(kmean_ref[:, hs], q_t[hs, :], HIGHEST)
        cur = jnp.where(past, gate, -jnp.inf)
        for _ in range(MOBA_TOPK - 1):
            cur = jnp.where(cur == jnp.max(cur, axis=0, keepdims=True), -jnp.inf, cur)
        thr = jnp.max(cur, axis=0, keepdims=True)
        selb_ref[h] = jnp.where(past & (gate >= thr), 0.0, MASKED)

    def score_dot(h, hs, rows, nblk):
        sc_ref[h, :nblk * BL, :] = _dot(k_ref[rows, hs], qt_ref[hs, :])

    def chunk(start, nblk, biases, ms, ls, first, scores_ready, next_start):
        rows = pl.ds(pl.multiple_of(start, BL), nblk * BL)
        next_rows = pl.ds(pl.multiple_of(next_start, BL), 2 * BL)
        if not scores_ready:
            for h, hs in enumerate(heads):
                score_dot(h, hs, rows, nblk)
        ms_new, ls_new, pending = [], [], []
        for h, hs in enumerate(heads):
            scs = []
            m = ms[h]
            for t in range(nblk):
                sc = sc_ref[h, t * BL:(t + 1) * BL, :]
                if biases[h][t].shape[0] == 1:
                    m = jnp.maximum(m, jnp.max(sc, axis=0, keepdims=True) + biases[h][t])
                else:
                    sc = sc + biases[h][t]
                    m = jnp.maximum(m, jnp.max(sc, axis=0, keepdims=True))
                scs.append(sc)
            psum = jnp.zeros((1, BL), F32)
            for t, sc in enumerate(scs):
                shift = m - biases[h][t] if biases[h][t].shape[0] == 1 else m
                p = jnp.exp2(sc - shift)
                psum = psum + jnp.sum(p, axis=0, keepdims=True)
                p_ref[h, t * BL:(t + 1) * BL, :] = p.astype(BF16)
            score_dot(h, hs, next_rows, 2)
            pv = _dot(vt_ref[hs, rows], p_ref[h, :nblk * BL, :])
            alpha = None if first else jnp.exp2(ms[h] - m)
            ls_new.append(psum if first else alpha * ls[h] + psum)
            ms_new.append(m)
            for hsp, pvp, alphap in pending:
                acc_ref[hsp, :] = pvp if first else alphap * acc_ref[hsp, :] + pvp
            pending = [(hs, pv, alpha)]
        for hsp, pvp, alphap in pending:
            acc_ref[hsp, :] = pvp if first else alphap * acc_ref[hsp, :] + pvp
        return tuple(ms_new), tuple(ls_new)

    init = tuple(jnp.full((1, BL), MASKED, F32) for _ in heads)

    def first_single():
        return chunk(i * BL, 1, [[own_bias]] * MOBA_HEADS, init, None, True, False, 0)

    def first_pair():
        biases = [[selb_ref[h, pl.ds(i - 1, 1), :], own_bias] for h in range(MOBA_HEADS)]
        return chunk((i - 1) * BL, 2, biases, init, None, True, False, 0)

    carry = lax.cond(i % 2 == 1, first_pair, first_single)

    def body(n, carry):
        biases = [[selb_ref[h, pl.ds(2 * n, 1), :], selb_ref[h, pl.ds(2 * n + 1, 1), :]]
                  for h in range(MOBA_HEADS)]
        nxt = jnp.minimum(2 * n + 2, nb - 2) * BL
        return chunk(2 * n * BL, 2, biases, carry[0], carry[1], False, True, nxt)

    ms, ls = lax.fori_loop(0, i // 2, body, carry)
    for h, hs in enumerate(heads):
        acc_ref[hs, :] = acc_ref[hs, :] / ls[h]
    o_ref[...] = _rms_rows(acc_ref[...].T, ng_ref[...]).astype(BF16)


def moba_attention(mqkv, norm_g):
    b, s, _ = mqkv.shape
    BL = MOBA_BLOCK
    nb = s // BL
    return pl.pallas_call(
        _moba_kernel,
        grid=(b, nb),
        in_specs=[
            pl.BlockSpec((None, BL, MOBA_WIDTH), lambda bi, i: (bi, i, 0)),
            pl.BlockSpec((None, s, MOBA_WIDTH), lambda bi, i: (bi, 0, 1)),
            pl.BlockSpec((None, s, MOBA_WIDTH), lambda bi, i: (bi, 0, 2)),
            pl.BlockSpec((1, MOBA_WIDTH), lambda bi, i: (0, 0)),
        ],
        out_specs=pl.BlockSpec((None, BL, MOBA_WIDTH), lambda bi, i: (bi, i, 0)),
        out_shape=jax.ShapeDtypeStruct((b, s, MOBA_WIDTH), BF16),
        scratch_shapes=[pltpu.VMEM((nb, MOBA_WIDTH), F32), pltpu.VMEM((MOBA_WIDTH, s), BF16),
                        pltpu.VMEM((MOBA_WIDTH, BL), BF16), pltpu.VMEM((MOBA_HEADS, nb, BL), F32),
                        pltpu.VMEM((MOBA_HEADS, 2 * BL, BL), F32), pltpu.VMEM((MOBA_HEADS, 2 * BL, BL), BF16),
                        pltpu.VMEM((MOBA_WIDTH, BL), F32)],
        compiler_params=_cparams(("arbitrary", "arbitrary")),
        name="moba_attention",
    )(mqkv, mqkv, mqkv, norm_g)


SWA_QBLOCKS = 4


def _swa_kernel(sink_ref, q_ref, kp_ref, kc_ref, vp_ref, vc_ref, ng_ref, o_ref, sc_ref, out_t_ref):
    W, Dh, NQ = SWA_BLOCK, HEAD_DIM, SWA_QBLOCKS
    i = pl.program_id(1)
    key_i = lax.broadcasted_iota(jnp.int32, (2 * W, W), 0)
    qry_i = lax.broadcasted_iota(jnp.int32, (2 * W, W), 1)
    rel = qry_i + W - key_i
    in_win = (rel >= 0) & (rel < W)
    bias_rest = jnp.where(in_win, 0.0, MASKED)
    bias_first = jnp.where(in_win & ((key_i >= W) | (i > 0)), 0.0, MASKED)
    rep = SWA_HEADS // SWA_KV_HEADS
    q_t = (q_ref[...].astype(F32) * (Dh ** -0.5 * LOG2E)).T.astype(BF16)

    def band(prev_ref, cur_ref, t, gs):
        if t == 0:
            return jnp.concatenate([prev_ref[:, gs], cur_ref[0:W, gs]], axis=0)
        return cur_ref[(t - 1) * W:(t + 1) * W, gs]

    for t in range(NQ):
        for g in range(SWA_KV_HEADS):
            k_band = band(kp_ref, kc_ref, t, slice(g * Dh, (g + 1) * Dh))
            for r in range(rep):
                hq = g * rep + r
                sc_ref[t * SWA_HEADS + hq] = _dot(k_band, q_t[hq * Dh:(hq + 1) * Dh, t * W:(t + 1) * W])
    for t in range(NQ):
        for hq in range(SWA_HEADS):
            g = hq // rep
            hs = slice(hq * Dh, (hq + 1) * Dh)
            sc = sc_ref[t * SWA_HEADS + hq] + (bias_first if t == 0 else bias_rest)
            sink = sink_ref[hq] * LOG2E
            m = jnp.maximum(jnp.max(sc, axis=0, keepdims=True), sink)
            p = jnp.exp2(sc - m)
            den = jnp.sum(p, axis=0, keepdims=True) + jnp.exp2(sink - m)
            v_band = band(vp_ref, vc_ref, t, slice(g * Dh, (g + 1) * Dh))
            out_t_ref[hs, t * W:(t + 1) * W] = _dot_tn(v_band, p.astype(BF16)) / den
    o_ref[...] = _rms_rows(out_t_ref[...].T, ng_ref[...]).astype(BF16)


def swa_attention(sqkv, sinks, norm_g):
    b, s, _ = sqkv.shape
    W, NQ = SWA_BLOCK, SWA_QBLOCKS
    tq = NQ * W
    prev = lambda bi, i: jnp.maximum(i * NQ - 1, 0)
    return pl.pallas_call(
        _swa_kernel,
        grid=(b, s // tq),
        in_specs=[
            pl.BlockSpec(memory_space=pltpu.SMEM),
            pl.BlockSpec((None, tq, SWA_WIDTH), lambda bi, i: (bi, i, 0)),
            pl.BlockSpec((None, W, SWA_KV_WIDTH), lambda bi, i: (bi, prev(bi, i), 2)),
            pl.BlockSpec((None, tq, SWA_KV_WIDTH), lambda bi, i: (bi, i, 2)),
            pl.BlockSpec((None, W, SWA_KV_WIDTH), lambda bi, i: (bi, prev(bi, i), 3)),
            pl.BlockSpec((None, tq, SWA_KV_WIDTH), lambda bi, i: (bi, i, 3)),
            pl.BlockSpec((1, SWA_WIDTH), lambda bi, i: (0, 0)),
        ],
        out_specs=pl.BlockSpec((None, tq, SWA_WIDTH), lambda bi, i: (bi, i, 0)),
        out_shape=jax.ShapeDtypeStruct((b, s, SWA_WIDTH), BF16),
        scratch_shapes=[pltpu.VMEM((NQ * SWA_HEADS, 2 * W, W), F32), pltpu.VMEM((SWA_WIDTH, tq), F32)],
        compiler_params=_cparams(("arbitrary", "arbitrary")),
        name="swa_attention",
    )(sinks, sqkv, sqkv, sqkv, sqkv, sqkv, norm_g)


def _outproj_kernel(x_ref, ys_ref, ym_ref, yw_ref, w_ref, mod_ref, g_ref, xo_ref, h_ref, ht_ref):
    ycat = jnp.concatenate([ys_ref[...], ym_ref[...], yw_ref[...]], axis=-1)
    xn = x_ref[...] + mod_ref[2:3, :] * _dot(ycat, w_ref[...])
    xo_ref[...] = xn
    h = _rms_rows(xn, g_ref[...]) * (1.0 + mod_ref[4:5, :]) + mod_ref[3:4, :]
    h_ref[...] = h.astype(BF16)
    ht_ref[...] = h.T.astype(BF16)


def out_proj(x, y_ssd, y_moba, y_swa, w_out, mod, g, tm=512):
    b, s, d = x.shape
    nt = s // tm
    tok = lambda n: pl.BlockSpec((None, tm, n), lambda bi, i: (bi, i, 0))
    return pl.pallas_call(
        _outproj_kernel,
        grid=(b, nt),
        in_specs=[tok(d), tok(SSD_WIDTH), tok(MOBA_WIDTH), tok(SWA_WIDTH),
                  pl.BlockSpec((d, d), lambda bi, i: (0, 0)),
                  pl.BlockSpec((None, 6, d), lambda bi, i: (bi, 0, 0)),
                  pl.BlockSpec((1, d), lambda bi, i: (0, 0))],
        out_specs=[tok(d), tok(d), pl.BlockSpec((d, tm), lambda bi, i: (0, bi * nt + i))],
        out_shape=[jax.ShapeDtypeStruct((b, s, d), F32), jax.ShapeDtypeStruct((b, s, d), BF16),
                   jax.ShapeDtypeStruct((d, b * s), BF16)],
        compiler_params=_cparams(("arbitrary", "arbitrary")),
        name="out_proj",
    )(x, y_ssd, y_moba, y_swa, w_out, mod, g)


_CAND = [(r, c) for r in range(PEER_TOPK) for c in range(PEER_TOPK) if (r + 1) * (c + 1) <= PEER_TOPK]
_CAND_ROWS = -(-len(_CAND) // 8) * 8
_NL_SPLIT = 4
_NL_TOP = PEER_TOPK // (_NL_SPLIT + 1)


def _oddeven_merge_sort_pairs(n):
    pairs = []
    p = 1
    while p < n:
        k = p
        while k >= 1:
            for j in range(k % p, n - k, 2 * k):
                for i in range(min(k, n - j - k)):
                    if (i + j) // (2 * p) == (i + j + k) // (2 * p):
                        pairs.append((i + j, i + j + k))
            k //= 2
        p *= 2
    return pairs


_SUBLANES = 8
_SORT16 = _oddeven_merge_sort_pairs(PEER_TOPK)


def _exchange(vals, x, y):
    vals[x], vals[y] = jnp.maximum(vals[x], vals[y]), jnp.minimum(vals[x], vals[y])


def _tiles(s_t):
    r = s_t.reshape(s_t.shape[0] // _SUBLANES, _SUBLANES, s_t.shape[1])
    return [r[k] for k in range(r.shape[0])]


def _top16_sorted(s_t):
    vals = _tiles(s_t)
    for x, y in _SORT16:
        _exchange(vals, x, y)
    shift = _SUBLANES // 2
    while shift >= 1:
        other = [pltpu.roll(v, shift, 0) for v in vals]
        vals = [jnp.maximum(vals[k], other[PEER_TOPK - 1 - k]) for k in range(PEER_TOPK)]
        d = PEER_TOPK // 2
        while d >= 1:
            for k in range(PEER_TOPK):
                if k & d == 0:
                    _exchange(vals, k, k + d)
            d //= 2
        shift //= 2
    return vals


def _rank_in(x, top):
    def pick(bits, lo, hi):
        if not bits:
            return top[(lo + hi) // 2]
        mid = (lo + hi) // 2
        return jnp.where(bits[0], pick(bits[1:], mid + 1, hi), pick(bits[1:], lo, mid - 1))

    bits = []
    rank = jnp.zeros(x.shape, F32)
    step = PEER_TOPK // 2
    while step >= 1:
        below = x < pick(bits, 0, PEER_TOPK - 2)
        rank = rank + jnp.where(below, float(step), 0.0)
        bits.append(below)
        step //= 2
    return jnp.where(x < top[PEER_TOPK - 1], float(PEER_TOPK), rank)


_ROUTE_LANES = 128


def _route_kernel(h_ref, wq_ref, k1_ref, k2_ref, nl_ref, c1_ref, r2_ref, e2_ref, cand_ref):
    tm = h_ref.shape[0]
    q = _dot(h_ref[...], wq_ref[...])
    k1 = k1_ref[...]
    k2 = k2_ref[...]
    cand_ref[len(_CAND):, :] = jnp.full((_CAND_ROWS - len(_CAND), _ROUTE_LANES), -jnp.inf, F32)
    for h in range(PEER_HEADS):
        q1 = q[:, (2 * h) * PEER_HALF:(2 * h + 1) * PEER_HALF].astype(BF16)
        q2 = q[:, (2 * h + 1) * PEER_HALF:(2 * h + 2) * PEER_HALF].astype(BF16)
        s1_all = _dot_nt(k1, q1)
        s2_all = _dot_nt(k2, q2)
        for lt in range(tm // _ROUTE_LANES):
            lanes = slice(lt * _ROUTE_LANES, (lt + 1) * _ROUTE_LANES)
            s1 = s1_all[:, lanes]
            s2 = s2_all[:, lanes]
            t1 = _top16_sorted(s1)
            t2 = _top16_sorted(s2)
            v1 = [t[0:1, :] for t in t1]
            v2 = [t[0:1, :] for t in t2]
            rank2 = jnp.concatenate([_rank_in(x, t2) for x in _tiles(s2)], axis=0)
            for idx, (r, c) in enumerate(_CAND):
                cand_ref[idx:idx + 1, :] = v1[r] + v2[c]
            cand = cand_ref[...]
            cur = cand
            for _ in range(PEER_TOPK - 1):
                cur = jnp.where(cur == jnp.max(cur, axis=0, keepdims=True), -jnp.inf, cur)
            theta = jnp.max(cur, axis=0, keepdims=True)
            top = v1[0] + v2[0]
            zsum = jnp.sum(jnp.where(cand >= theta, jnp.exp(cand - top), 0.0), axis=0, keepdims=True)
            nlim = jnp.zeros(s1.shape, F32)
            for c in range(_NL_SPLIT):
                nlim = nlim + jnp.where(s1 + v2[c] >= theta, 1.0, 0.0)
            extra = jnp.zeros(s1.shape, F32)
            for r in reversed(range(_NL_TOP)):
                tail = jnp.zeros_like(theta)
                for c in range(_NL_SPLIT, PEER_TOPK):
                    if (r + 1) * (c + 1) <= PEER_TOPK:
                        tail = tail + jnp.where(v1[r] + v2[c] >= theta, 1.0, 0.0)
                extra = jnp.where(s1 == v1[r], tail, extra)
            nl_ref[h, :, lanes] = nlim + extra
            c1_ref[h, :, lanes] = jnp.exp(s1 - v1[0]) * (0.5 / zsum)
            r2_ref[h, :, lanes] = rank2.astype(BF16)
            e2_ref[h, :, lanes] = jnp.exp(s2 - v2[0]).astype(BF16)


def peer_route(h2, wq, k1, k2, tm=256):
    t, d = h2.shape
    big = lambda: pl.BlockSpec((PEER_HEADS, PEER_NKEYS, tm), lambda i: (0, 0, i))
    bigs = lambda dt: jax.ShapeDtypeStruct((PEER_HEADS, PEER_NKEYS, t), dt)
    return pl.pallas_call(
        _route_kernel,
        grid=(t // tm,),
        in_specs=[
            pl.BlockSpec((tm, d), lambda i: (i, 0)),
            pl.BlockSpec((d, 2 * PEER_HEADS * PEER_HALF), lambda i: (0, 0)),
            pl.BlockSpec((PEER_NKEYS, PEER_HALF), lambda i: (0, 0)),
            pl.BlockSpec((PEER_NKEYS, PEER_HALF), lambda i: (0, 0)),
        ],
        out_specs=[big(), big(), big(), big()],
        out_shape=[bigs(F32), bigs(F32), bigs(BF16), bigs(BF16)],
        scratch_shapes=[pltpu.VMEM((_CAND_ROWS, _ROUTE_LANES), F32)],
        compiler_params=_cparams(("arbitrary",)),
        name="peer_route",
    )(h2, wq, k1, k2)


_SLAB = 16
_LANES = 256
PEER_TM = 512
PEER_TE = 2048
PEER_SUB = 512
_PRE_AHEAD = 2


def _peer_kernel(ht_ref, u_ref, vt_ref, nl_ref, c1_ref, r2_ref, e2_ref, x_ref, mod_ref,
                 o_ref, pre_ref, act_ref, acc_ref):
    j = pl.program_id(1)
    tm = ht_ref.shape[1]
    te = u_ref.shape[0]
    groups = PEER_SUB // PEER_NKEYS

    @pl.when(j == 0)
    def _():
        acc_ref[...] = jnp.zeros_like(acc_ref)

    nsub = te // PEER_SUB
    sub_rows = [slice(sc * PEER_SUB, (sc + 1) * PEER_SUB) for sc in range(nsub)]
    def acc_dot(sc):
        acc_ref[...] = acc_ref[...] + _dot(vt_ref[:, sub_rows[sc]], act_ref[sc])

    def pre_dot(sc):
        pre_ref[sc] = _dot(u_ref[sub_rows[sc], :], ht_ref[...])

    for sc in range(min(_PRE_AHEAD, nsub)):
        pre_dot(sc)
    for sc in range(nsub):
        if sc + _PRE_AHEAD < nsub:
            pre_dot(sc + _PRE_AHEAD)
        if sc >= 1:
            acc_dot(sc - 1)
        for gi in range(groups):
            local = sc * groups + gi
            base8 = pl.multiple_of(j * (te // PEER_NKEYS) + (local // 8) * 8, 8)
            r8 = local % 8
            for lt in range(tm // _LANES):
                lanes = slice(lt * _LANES, (lt + 1) * _LANES)
                nslab = PEER_NKEYS // _SLAB
                slabs = [slice(sl * _SLAB, (sl + 1) * _SLAB) for sl in range(nslab)]
                gsum = [jnp.zeros((_SLAB, _LANES), BF16)] * nslab
                for h in range(PEER_HEADS):
                    nl_row = jnp.broadcast_to(nl_ref[h, pl.ds(base8, 8), lanes][r8:r8 + 1, :],
                                              (_SLAB, _LANES)).astype(BF16)
                    c1_row = jnp.broadcast_to(c1_ref[h, pl.ds(base8, 8), lanes][r8:r8 + 1, :],
                                              (_SLAB, _LANES)).astype(BF16)
                    for sl, rows in enumerate(slabs):
                        hit = r2_ref[h, rows, lanes] < nl_row
                        gsum[sl] = gsum[sl] + jnp.where(hit, e2_ref[h, rows, lanes], jnp.zeros((), BF16)) * c1_row
                for sl in range(nslab):
                    prow = slice(gi * PEER_NKEYS + sl * _SLAB, gi * PEER_NKEYS + (sl + 1) * _SLAB)
                    pre = pre_ref[sc, prow, lanes]
                    gelu2 = pre * (1.0 + lax.erf(pre * (2.0 ** -0.5)))
                    act_ref[sc, prow, lanes] = gelu2.astype(BF16) * gsum[sl]
    acc_dot(nsub - 1)

    @pl.when(j == pl.num_programs(1) - 1)
    def _():
        o_ref[...] = x_ref[...] + mod_ref[5:6, :] * acc_ref[...].T


def peer_experts(h2t, u, vt, nl, c1, r2, e2, x, mod, seq, tm=PEER_TM, te=PEER_TE):
    d, t = h2t.shape
    ne = u.shape[0]
    big = lambda: pl.BlockSpec((PEER_HEADS, PEER_NKEYS, tm), lambda i, j: (0, 0, i))
    nsub = te // PEER_SUB
    return pl.pallas_call(
        _peer_kernel,
        grid=(t // tm, ne // te),
        in_specs=[
            pl.BlockSpec((d, tm), lambda i, j: (0, i)),
            pl.BlockSpec((te, d), lambda i, j: (j, 0)),
            pl.BlockSpec((d, te), lambda i, j: (0, j)),
            big(), big(), big(), big(),
            pl.BlockSpec((tm, d), lambda i, j: (i, 0)),
            pl.BlockSpec((None, 6, d), lambda i, j: ((i * tm) // seq, 0, 0)),
        ],
        out_specs=pl.BlockSpec((tm, d), lambda i, j: (i, 0)),
        out_shape=jax.ShapeDtypeStruct((t, d), F32),
        scratch_shapes=[pltpu.VMEM((nsub, PEER_SUB, tm), F32), pltpu.VMEM((nsub, PEER_SUB, tm), BF16),
                        pltpu.VMEM((d, tm), F32)],
        compiler_params=_cparams(("arbitrary", "arbitrary")),
        name="peer_experts",
    )(h2t, u, vt, nl, c1, r2, e2, x, mod)


def _final_kernel(x_ref, g_ref, o_ref):
    o_ref[...] = _rms_rows(x_ref[...], g_ref[...])


def final_norm(x, g, tm=1024):
    t, d = x.shape
    return pl.pallas_call(
        _final_kernel,
        grid=(t // tm,),
        in_specs=[pl.BlockSpec((tm, d), lambda i: (i, 0)), pl.BlockSpec((1, d), lambda i: (0, 0))],
        out_specs=pl.BlockSpec((tm, d), lambda i: (i, 0)),
        out_shape=jax.ShapeDtypeStruct((t, d), F32),
        compiler_params=_cparams(("arbitrary",)),
        name="final_norm",
    )(x, g)


def _reorder_w_in(w_in):
    o_dt = SSD_WIDTH + XBC_WIDTH
    o_rest = o_dt + SSD_HEADS
    pad = jnp.zeros(w_in.shape[:-1] + (DT_PAD - SSD_HEADS,), w_in.dtype)
    return jnp.concatenate([w_in[..., :o_dt], w_in[..., o_rest:], w_in[..., o_dt:o_rest], pad], axis=-1)


def _pad_heads(p):
    return jnp.pad(p, ((0, 0), (0, DT_PAD - SSD_HEADS)))


def kernel(x, c, ada_w, ada_b, norm1_g, norm2_g, w_in, conv_w, conv_b, dt_bias, a_log, d_skip, ssd_norm_g, moba_norm_g, swa_sinks, swa_norm_g, w_out, peer_wq, peer_k1, peer_k2, peer_u, peer_v, final_g):
    b, s, d = x.shape
    depth = ada_w.shape[0]
    mod_all = ada_mod(c, ada_w, ada_b).reshape(depth, b, 6, d)
    w_in_b = _reorder_w_in(w_in.astype(BF16))
    w_out_b = w_out.astype(BF16)
    wq_b = peer_wq.astype(BF16)
    k1_b = peer_k1.astype(BF16)
    k2_b = peer_k2.astype(BF16)
    dtb = _pad_heads(dt_bias)
    alog = _pad_heads(a_log)
    dskip_x = jnp.repeat(d_skip, HEAD_DIM, axis=-1)
    for l in range(depth):
        mod = mod_all[l]
        z, xbc, dt, mqkv, sqkv = in_proj(x, mod, norm1_g[l][None], w_in_b[l])
        y_ssd = ssd_mixer(z, xbc, dt, conv_w[l], conv_b[l][None], dtb[l][None], alog[l][None],
                          dskip_x[l][None], ssd_norm_g[l][None])
        y_moba = moba_attention(mqkv, moba_norm_g[l][None])
        y_swa = swa_attention(sqkv, swa_sinks[l], swa_norm_g[l][None])
        x, h2, h2t = out_proj(x, y_ssd, y_moba, y_swa, w_out_b[l], mod, norm2_g[l][None])
        nl, c1, r2, e2 = peer_route(h2.reshape(b * s, d), wq_b[l], k1_b[l], k2_b[l])
        x = peer_experts(h2t, peer_u[l].astype(BF16), peer_v[l].T.astype(BF16), nl, c1, r2, e2,
                         x.reshape(b * s, d), mod, s).reshape(b, s, d)
    return final_norm(x.reshape(b * s, d), final_g[None]).reshape(b, s, d)
```
